```python
import math
import jax, jax.numpy as jnp
from jax import lax
import numpy as np

D_MODEL = 1024
BATCH = 8
SEQ = 2048
DEPTH = 4
DEC_BATCH = 128
DEC_SEQ = 1
PAST_LEN = 16384
PAGE_SIZE = 128

H_A = 4
HD_A = 64
D_A = H_A * HD_A
CHUNK = 128
H_B = 4
DK_B = 64
DV_B = 128
D_BQK = H_B * DK_B
D_BV = H_B * DV_B
H_C = 4
HD_C = 64
D_C = H_C * HD_C
LORA_W = 64
LORA_A = 64
LORA_G = 128
C_IN = 3 * D_C + LORA_W + LORA_A + LORA_G
D_MIX = D_A + D_BV + D_C
D_IN = 2 * D_A + 2 * D_BQK + 2 * D_BV + C_IN
D_FF = 2816
N_MOD = 9
ROPE_BASE = 10000.0
NORM_EPS = 1e-6
GN_EPS = 64e-5

kernel_name = 'hybrid_sgmlp_retention_rwkv7_step'


def _split_points(sizes):
    pts, acc = [], 0
    for s in sizes[:-1]:
        acc += s
        pts.append(acc)
    return pts


def rms_norm(x, eps=NORM_EPS):
    xf = x.astype(jnp.float32)
    return (xf * lax.rsqrt(jnp.mean(xf * xf, axis=-1, keepdims=True) + eps)).astype(x.dtype)


def group_layer_norm(x, g, b, eps=GN_EPS):
    xf = x.astype(jnp.float32)
    mu = jnp.mean(xf, axis=-1, keepdims=True)
    xc = xf - mu
    var = jnp.mean(xc * xc, axis=-1, keepdims=True)
    return xc * lax.rsqrt(var + eps) * g + b


def swiglu(x, w_in, w_out):
    gate, up = jnp.split(x @ w_in, 2, axis=-1)
    return (jax.nn.silu(gate) * up) @ w_out


def rotary(x, pos):
    half = x.shape[-1] // 2
    inv = ROPE_BASE ** (-jnp.arange(half, dtype=jnp.float32) / half)
    ang = pos[:, None] * inv[None, :]
    cos = jnp.cos(ang)[None, :, None, :]
    sin = jnp.sin(ang)[None, :, None, :]
    x1, x2 = x[..., :half], x[..., half:]
    return jnp.concatenate([x1 * cos - x2 * sin, x1 * sin + x2 * cos], axis=-1)


def chunk_spatial_gating(u, v, v_gain, w_s, b_s):
    bsz, t = u.shape[:2]
    vn = rms_norm(v) * v_gain
    pad = (-t) % CHUNK
    vp = jnp.pad(vn, ((0, 0), (0, pad), (0, 0), (0, 0)))
    n_c = (t + pad) // CHUNK
    vp = vp.reshape(bsz, n_c, CHUNK, H_A, HD_A)
    causal = jnp.tril(jnp.ones((CHUNK, CHUNK), dtype=bool))
    w_m = jnp.where(causal[None], w_s, 0.0).astype(vp.dtype)
    mixed = jnp.einsum('hts,bnshd->bnthd', w_m, vp) + b_s.T[None, None, :, :, None]
    mixed = mixed.reshape(bsz, n_c * CHUNK, H_A, HD_A)[:, :t]
    return (u * mixed).reshape(bsz, t, D_A), vn


def retention_log_decay():
    return jnp.log1p(-jnp.exp(jnp.linspace(math.log(1.0 / 32.0), math.log(1.0 / 512.0), H_B, dtype=jnp.float32)))


def retention(q, k, v, s0, pos0):
    bsz, t = q.shape[:2]
    c = CHUNK if t % CHUNK == 0 else t
    n_c = t // c
    pos = pos0 + jnp.arange(t, dtype=jnp.float32)
    q = rotary(q.astype(jnp.float32), pos)
    k = rotary(k.astype(jnp.float32), pos) * (DK_B ** -0.5)
    v = v.astype(jnp.float32)
    lg = retention_log_decay()
    idx = jnp.arange(c, dtype=jnp.float32)
    diff = idx[:, None] - idx[None, :]
    dmask = jnp.where(diff[None] >= 0, jnp.exp(jnp.maximum(diff, 0.0)[None] * lg[:, None, None]), 0.0)
    q_dec = jnp.exp((idx[:, None] + 1.0) * lg[None, :])
    k_dec = jnp.exp((c - 1.0 - idx)[:, None] * lg[None, :])
    s_dec = jnp.exp(c * lg)

    def to_chunks(a):
        return a.reshape(bsz, n_c, c, H_B, a.shape[-1]).swapaxes(0, 1)

    def step(s, xs):
        qc, kc, vc = xs
        scores = jnp.einsum('bihd,bjhd->bhij', qc, kc) * dmask
        inner = jnp.einsum('bhij,bjhe->bihe', scores, vc)
        cross = jnp.einsum('bihd,bhde->bihe', qc, s) * q_dec[None, :, :, None]
        s_new = s * s_dec[None, :, None, None] + jnp.einsum('bjhd,bjhe->bhde', kc * k_dec[None, :, :, None], vc)
        return s_new, inner + cross

    s_fin, out = lax.scan(step, s0.astype(jnp.float32), (to_chunks(q), to_chunks(k), to_chunks(v)))
    out = out.swapaxes(0, 1).reshape(bsz, t, H_B, DV_B)
    return out, s_fin


def rwkv7_mix(zc, prev, mu, w0, w2, a0, a2, g2, k_k, k_a, r_k, ln_g, ln_b, s0):
    bsz, t = zc.shape[:2]
    shifted = jnp.concatenate([prev[:, None].astype(zc.dtype), zc[:, :-1]], axis=1)
    zs = zc + (shifted - zc) * mu
    r, k, v, xw, xa, xg = jnp.split(zs, _split_points([D_C, D_C, D_C, LORA_W, LORA_A, LORA_G]), axis=-1)
    w_log = -jax.nn.softplus(-(w0 + jnp.tanh(xw) @ w2)) - 0.5
    decay = jnp.exp(-jnp.exp(w_log.astype(jnp.float32)))
    a = jax.nn.sigmoid(a0 + xa @ a2)
    g = jax.nn.sigmoid(xg) @ g2

    def heads(z):
        return z.reshape(bsz, t, H_C, HD_C).astype(jnp.float32)

    kk = heads(k * k_k)
    kk = kk / jnp.maximum(jnp.sqrt(jnp.sum(kk * kk, axis=-1, keepdims=True)), 1e-12)
    k = k * (1.0 + (a - 1.0) * k_a)
    r_h, k_h, v_h, a_h, w_h = heads(r), heads(k), heads(v), heads(a), heads(decay)

    def step(s, xs):
        r_t, w_t, k_t, v_t, kk_t, a_t = xs
        sa = jnp.einsum('bhvk,bhk->bhv', s, -kk_t)
        s = s * w_t[:, :, None, :] + sa[..., None] * (kk_t * a_t)[:, :, None, :] + v_t[..., None] * k_t[:, :, None, :]
        return s, jnp.einsum('bhvk,bhk->bhv', s, r_t)

    def tm(z):
        return z.swapaxes(0, 1)

    s_fin, y = lax.scan(step, s0.astype(jnp.float32), (tm(r_h), tm(w_h), tm(k_h), tm(v_h), tm(kk), tm(a_h)))
    y = group_layer_norm(tm(y), ln_g, ln_b)
    bonus = jnp.sum(r_h * k_h * r_k, axis=-1, keepdims=True) * v_h
    out = (y + bonus).reshape(bsz, t, D_C) * g.astype(jnp.float32)
    return out.astype(zc.dtype), s_fin, zc[:, -1]


def run_trunk(x, c, pos0, ret0, wkv0, shift0, p):
    bsz, t, _ = x.shape
    cs = jax.nn.silu(c)
    rets, wkvs, shifts, vrows = [], [], [], []
    for l in range(DEPTH):
        mod = cs @ p['w_ada'][l] + p['b_ada'][l]
        sh1, sc1, g1, shm, scm, gm, sh2, sc2, g2 = [m[:, None, :] for m in jnp.split(mod, N_MOD, axis=-1)]
        h = rms_norm(x) * (1.0 + sc1) + sh1
        x = x + 0.5 * g1 * swiglu(h, p['w_ffn1_in'][l], p['w_ffn1_out'][l])
        h = rms_norm(x) * (1.0 + scm) + shm
        z = h @ p['w_in'][l]
        zu, zv, zq, zk, zrv, zg, zc = jnp.split(z, _split_points([D_A, D_A, D_BQK, D_BQK, D_BV, D_BV, C_IN]), axis=-1)
        ya, va = chunk_spatial_gating(
            jax.nn.gelu(zu).reshape(bsz, t, H_A, HD_A),
            jax.nn.gelu(zv).reshape(bsz, t, H_A, HD_A),
            p['sg_v_gain'][l], p['sg_w_s'][l], p['sg_b_s'][l])
        ob, s_ret = retention(zq.reshape(bsz, t, H_B, DK_B), zk.reshape(bsz, t, H_B, DK_B),
                              zrv.reshape(bsz, t, H_B, DV_B), ret0[l], pos0)
        yb = (rms_norm(ob) * p['ret_gn_g'][l]).reshape(bsz, t, D_BV).astype(x.dtype) * jax.nn.silu(zg)
        yc, s_wkv, last = rwkv7_mix(zc, shift0[l], p['rw_mu'][l], p['rw_w0'][l], p['rw_w2'][l],
                                    p['rw_a0'][l], p['rw_a2'][l], p['rw_g2'][l], p['rw_k_k'][l],
                                    p['rw_k_a'][l], p['rw_r_k'][l], p['rw_ln_g'][l], p['rw_ln_b'][l], wkv0[l])
        y_mix = jnp.concatenate([ya.astype(x.dtype), yb, yc.astype(x.dtype)], axis=-1) @ p['w_out'][l]
        x = x + gm * y_mix
        h = rms_norm(x) * (1.0 + sc2) + sh2
        x = x + 0.5 * g2 * swiglu(h, p['w_ffn2_in'][l], p['w_ffn2_out'][l])
        rets.append(s_ret)
        wkvs.append(s_wkv)
        shifts.append(last)
        vrows.append(va.reshape(bsz, t, D_A))
    y = rms_norm(x) * p['final_g']
    return y, jnp.stack(rets), jnp.stack(wkvs), jnp.stack(shifts), vrows


def setup_inputs(seed: int = 0) -> dict:
    key = jax.random.key(seed)
    ks = jax.random.split(key, 40)
    f32 = jnp.float32

    def nrm(i, shape, scale):
        return jax.random.normal(ks[i], shape, f32) * scale

    def uni(i, shape, lo, hi):
        return jax.random.uniform(ks[i], shape, f32, lo, hi)

    return {
        'x_prompt': nrm(0, (BATCH, SEQ, D_MODEL), 1.0),
        'x_sample': nrm(1, (DEC_BATCH, DEC_SEQ, D_MODEL), 1.0),
        'state_ret': nrm(2, (DEPTH, DEC_BATCH, H_B, DK_B, DV_B), 0.1),
        'state_wkv': nrm(3, (DEPTH, DEC_BATCH, H_C, HD_C, HD_C), 0.1),
        'state_shift': nrm(4, (DEPTH, DEC_BATCH, C_IN), 1.0),
        'c_prompt': nrm(5, (BATCH, D_MODEL), 1.0),
        'c_sample': nrm(6, (DEC_BATCH, D_MODEL), 1.0),
        'w_ada': nrm(7, (DEPTH, D_MODEL, N_MOD * D_MODEL), 0.5 * D_MODEL ** -0.5),
        'b_ada': nrm(8, (DEPTH, N_MOD * D_MODEL), 0.02),
        'w_ffn1_in': nrm(9, (DEPTH, D_MODEL, 2 * D_FF), D_MODEL ** -0.5),
        'w_ffn1_out': nrm(10, (DEPTH, D_FF, D_MODEL), D_FF ** -0.5),
        'w_in': nrm(11, (DEPTH, D_MODEL, D_IN), D_MODEL ** -0.5),
        'w_out': nrm(12, (DEPTH, D_MIX, D_MODEL), D_MIX ** -0.5),
        'w_ffn2_in': nrm(13, (DEPTH, D_MODEL, 2 * D_FF), D_MODEL ** -0.5),
        'w_ffn2_out': nrm(14, (DEPTH, D_FF, D_MODEL), D_FF ** -0.5),
        'sg_v_gain': 1.0 + nrm(15, (DEPTH, H_A, HD_A), 0.02),
        'sg_w_s': nrm(16, (DEPTH, H_A, CHUNK, CHUNK), CHUNK ** -0.5),
        'sg_b_s': 1.0 + nrm(17, (DEPTH, H_A, CHUNK), 0.02),
        'ret_gn_g': 1.0 + nrm(18, (DEPTH, H_B, DV_B), 0.02),
        'rw_mu': uni(19, (DEPTH, C_IN), 0.0, 1.0),
        'rw_w0': uni(20, (DEPTH, D_C), -6.5, 5.5),
        'rw_w2': nrm(21, (DEPTH, LORA_W, D_C), 0.1),
        'rw_a0': nrm(22, (DEPTH, D_C), 0.1),
        'rw_a2': nrm(23, (DEPTH, LORA_A, D_C), LORA_A ** -0.5),
        'rw_g2': nrm(24, (DEPTH, LORA_G, D_C), LORA_G ** -0.5),
        'rw_k_k': 0.85 + nrm(25, (DEPTH, D_C), 0.02),
        'rw_k_a': 1.0 + nrm(26, (DEPTH, D_C), 0.02),
        'rw_r_k': nrm(27, (DEPTH, H_C, HD_C), 0.1),
        'rw_ln_g': 1.0 + nrm(28, (DEPTH, H_C, HD_C), 0.02),
        'rw_ln_b': nrm(29, (DEPTH, H_C, HD_C), 0.02),
        'final_g': 1.0 + nrm(30, (D_MODEL,), 0.02),
    }


def reference(x_prompt, x_sample, state_ret, state_wkv, state_shift, c_prompt, c_sample,
              w_ada, b_ada, w_ffn1_in, w_ffn1_out, w_in, w_out, w_ffn2_in, w_ffn2_out,
              sg_v_gain, sg_w_s, sg_b_s, ret_gn_g,
              rw_mu, rw_w0, rw_w2, rw_a0, rw_a2, rw_g2, rw_k_k, rw_k_a, rw_r_k, rw_ln_g, rw_ln_b,
              final_g):
    p = {
        'w_ada': w_ada, 'b_ada': b_ada,
        'w_ffn1_in': w_ffn1_in, 'w_ffn1_out': w_ffn1_out,
        'w_in': w_in, 'w_out': w_out,
        'w_ffn2_in': w_ffn2_in, 'w_ffn2_out': w_ffn2_out,
        'sg_v_gain': sg_v_gain, 'sg_w_s': sg_w_s, 'sg_b_s': sg_b_s,
        'ret_gn_g': ret_gn_g,
        'rw_mu': rw_mu, 'rw_w0': rw_w0, 'rw_w2': rw_w2, 'rw_a0': rw_a0, 'rw_a2': rw_a2,
        'rw_g2': rw_g2, 'rw_k_k': rw_k_k, 'rw_k_a': rw_k_a, 'rw_r_k': rw_r_k,
        'rw_ln_g': rw_ln_g, 'rw_ln_b': rw_ln_b,
        'final_g': final_g,
    }
    bp = x_prompt.shape[0]
    ret0 = jnp.zeros((DEPTH, bp, H_B, DK_B, DV_B), jnp.float32)
    wkv0 = jnp.zeros((DEPTH, bp, H_C, HD_C, HD_C), jnp.float32)
    shift0 = jnp.zeros((DEPTH, bp, C_IN), x_prompt.dtype)
    y_prompt, ret_prompt, wkv_prompt, shift_prompt, _ = run_trunk(x_prompt, c_prompt, 0, ret0, wkv0, shift0, p)
    y_sample, ret_sample, wkv_sample, shift_sample, vrows_s = run_trunk(
        x_sample, c_sample, PAST_LEN, state_ret, state_wkv, state_shift, p)
    chunkv_sample = jnp.stack(vrows_s)
    return (y_prompt, y_sample, ret_prompt, wkv_prompt, shift_prompt, ret_sample, wkv_sample, shift_sample, chunkv_sample)
```

```python
import functools
import math

import jax
import jax.numpy as jnp
from jax import lax
from jax.experimental import pallas as pl
from jax.experimental.pallas import tpu as pltpu

F32 = jnp.float32
BF16 = jnp.bfloat16

D_MODEL = 1024
DEPTH = 4
N_MOD = 9
D_FF = 2816
N_HEADS = 4
HEAD_DIM = 64
DV_B = 128
CHUNK = 128
RW_CHUNK = 64
D_A = N_HEADS * HEAD_DIM
D_BV = N_HEADS * DV_B
D_C = N_HEADS * HEAD_DIM
C_IN = 1024
D_IN = 3072
Z_AB = 2048
D_AB = D_A + D_BV
ROPE_BASE = 10000.0
NORM_EPS = 1e-6
GN_EPS = 64e-5
PAST_LEN = 16384

VMEM_LIMIT_BYTES = 56 * 1024 * 1024
FFN_TILE_F = 1408


def _cparams(sem):
    return pltpu.CompilerParams(dimension_semantics=sem, vmem_limit_bytes=VMEM_LIMIT_BYTES)


def _dot(a, b):
    return jnp.dot(a.astype(BF16), b.astype(BF16), preferred_element_type=F32)


def _dot_t(a, b):
    return lax.dot_general(a.astype(BF16), b.astype(BF16), (((1,), (1,)), ((), ())),
                           preferred_element_type=F32)


def _split2(x):
    hi = x.astype(BF16)
    lo = (x - hi.astype(F32)).astype(BF16)
    return hi, lo


def _split3(x):
    hi = x.astype(BF16)
    r1 = x - hi.astype(F32)
    mid = r1.astype(BF16)
    lo = (r1 - mid.astype(F32)).astype(BF16)
    return hi, mid, lo


def _dot_sel(x, sel):
    hi, mid, lo = _split3(x)
    return (jnp.dot(hi, sel, preferred_element_type=F32)
            + jnp.dot(mid, sel, preferred_element_type=F32)
            + jnp.dot(lo, sel, preferred_element_type=F32))


def _dot3(a, b):
    ah, al = _split2(a)
    bh, bl = _split2(b)
    return (jnp.dot(ah, bh, preferred_element_type=F32)
            + jnp.dot(ah, bl, preferred_element_type=F32)
            + jnp.dot(al, bh, preferred_element_type=F32))


def _dot3_t(a, b):
    ah, al = _split2(a)
    bh, bl = _split2(b)
    dn = (((1,), (1,)), ((), ()))
    return (lax.dot_general(ah, bh, dn, preferred_element_type=F32)
            + lax.dot_general(ah, bl, dn, preferred_element_type=F32)
            + lax.dot_general(al, bh, dn, preferred_element_type=F32))


def _rms(x, eps=NORM_EPS):
    return x * lax.rsqrt(jnp.mean(x * x, axis=-1, keepdims=True) + eps)


def _silu(x):
    return x * jax.nn.sigmoid(x)


def _gelu(x):
    return jax.nn.gelu(x)


def _head_of(shape, dim, width):
    return lax.broadcasted_iota(jnp.int32, shape, dim) // width


def _group_sel(n_in, n_out, w_in, w_out, scale=1.0):
    gi = _head_of((n_in, n_out), 0, w_in)
    go = _head_of((n_in, n_out), 1, w_out)
    return jnp.where(gi == go, scale, 0.0).astype(BF16)


def _stack_heads(x, width):
    lane_head = _head_of(x.shape, 1, width)
    return jnp.concatenate([jnp.where(lane_head == h, x, 0.0) for h in range(N_HEADS)], axis=0)


def _rotate_half(x, cos_t, sin_t):
    n = x.shape[1]
    half = HEAD_DIM // 2
    lane = lax.broadcasted_iota(jnp.int32, x.shape, 1)
    fwd = pltpu.roll(x, n - half, axis=1)
    bwd = pltpu.roll(x, half, axis=1)
    partner = jnp.where((lane % HEAD_DIM) < half, fwd, bwd)
    return x * cos_t + partner * sin_t


def _ada_kernel(c_ref, w_ref, b_ref, o_ref):
    c = c_ref[...]
    o_ref[...] = _dot(_silu(c), w_ref[...]) + b_ref[...]


def _ada(c_all, w_ada, b_ada):
    bt = c_all.shape[0]
    n_out = N_MOD * D_MODEL
    tn = 1152
    return pl.pallas_call(
        _ada_kernel,
        grid=(DEPTH, n_out // tn),
        in_specs=[pl.BlockSpec((bt, D_MODEL), lambda l, j: (0, 0)),
                  pl.BlockSpec((None, D_MODEL, tn), lambda l, j: (l, 0, j)),
                  pl.BlockSpec((None, 1, tn), lambda l, j: (l, 0, j))],
        out_specs=pl.BlockSpec((None, bt, tn), lambda l, j: (l, 0, j)),
        out_shape=jax.ShapeDtypeStruct((DEPTH, bt, n_out), F32),
        compiler_params=_cparams(("parallel", "parallel")),
        name="ada_mod",
    )(c_all, w_ada, b_ada.reshape(DEPTH, 1, n_out))


class _Rows:
    def __init__(self, mod, n_rows, tile, rows_per_seq, per_row):
        self.mod, self.n_rows, self.tile, self.per_row = mod, n_rows, tile, per_row
        self.n_tiles = n_rows // tile
        self.tiles_per_seq = None if per_row else rows_per_seq // tile

    def mod_spec(self, j):
        if self.per_row:
            return pl.BlockSpec((None, self.tile, D_MODEL), lambda i, *_: (j, i, 0))
        tps = self.tiles_per_seq
        return pl.BlockSpec((None, None, 1, D_MODEL), lambda i, *_: (i // tps, j, 0, 0))

    def row_spec(self, width):
        return pl.BlockSpec((self.tile, width), lambda i, *_: (i, 0))


def _ffn_kernel(*refs, n_f, final):
    if final:
        x_ref, sh_ref, sc_ref, g_ref, wg_ref, wu_ref, wo_ref, fg_ref, o_ref, h_ref, acc_ref = refs
    else:
        x_ref, sh_ref, sc_ref, g_ref, wg_ref, wu_ref, wo_ref, o_ref, h_ref, acc_ref = refs
    j = pl.program_id(1)

    @pl.when(j == 0)
    def _():
        h_ref[...] = (_rms(x_ref[...]) * (1.0 + sc_ref[...]) + sh_ref[...]).astype(BF16)

    hb = h_ref[...]
    gate = jnp.dot(hb, wg_ref[...], preferred_element_type=F32)
    up = jnp.dot(hb, wu_ref[...], preferred_element_type=F32)
    y = jnp.dot((_silu(gate) * up).astype(BF16), wo_ref[...], preferred_element_type=F32)

    @pl.when(j == 0)
    def _():
        acc_ref[...] = y

    @pl.when(jnp.logical_and(j > 0, j < n_f - 1))
    def _():
        acc_ref[...] += y

    @pl.when(j == n_f - 1)
    def _():
        out = x_ref[...] + 0.5 * g_ref[...] * (acc_ref[...] + y)
        if final:
            out = _rms(out) * fg_ref[...]
        o_ref[...] = out


def _ffn(x, rows, l, jm, w_in_b, w_out_b, final_g=None):
    n_f = D_FF // FFN_TILE_F
    tf = FFN_TILE_F
    final = final_g is not None
    in_specs = [rows.row_spec(D_MODEL), rows.mod_spec(jm), rows.mod_spec(jm + 1), rows.mod_spec(jm + 2),
                pl.BlockSpec((None, D_MODEL, tf), lambda i, j: (l, 0, j)),
                pl.BlockSpec((None, D_MODEL, tf), lambda i, j: (l, 0, n_f + j)),
                pl.BlockSpec((None, tf, D_MODEL), lambda i, j: (l, j, 0))]
    args = [x, rows.mod, rows.mod, rows.mod, w_in_b, w_in_b, w_out_b]
    if final:
        in_specs.append(pl.BlockSpec((1, D_MODEL), lambda i, j: (0, 0)))
        args.append(final_g.reshape(1, D_MODEL))
    return pl.pallas_call(
        functools.partial(_ffn_kernel, n_f=n_f, final=final),
        grid=(rows.n_tiles, n_f),
        in_specs=in_specs,
        out_specs=rows.row_spec(D_MODEL),
        out_shape=jax.ShapeDtypeStruct((rows.n_rows, D_MODEL), F32),
        scratch_shapes=[pltpu.VMEM((rows.tile, D_MODEL), BF16), pltpu.VMEM((rows.tile, D_MODEL), F32)],
        compiler_params=_cparams(("parallel", "arbitrary")),
        name="ffn_final" if final else "ffn",
    )(*args)


def _proj_in_kernel(x_ref, sh_ref, sc_ref, w_ref, z_ref):
    h = (_rms(x_ref[...]) * (1.0 + sc_ref[...]) + sh_ref[...]).astype(BF16)
    z_ref[...] = jnp.dot(h, w_ref[...], preferred_element_type=F32)


def _proj_in(x, rows, l, w_in_b):
    return pl.pallas_call(
        _proj_in_kernel,
        grid=(rows.n_tiles,),
        in_specs=[rows.row_spec(D_MODEL), rows.mod_spec(3), rows.mod_spec(4),
                  pl.BlockSpec((None, D_MODEL, D_IN), lambda i: (l, 0, 0))],
        out_specs=rows.row_spec(D_IN),
        out_shape=jax.ShapeDtypeStruct((rows.n_rows, D_IN), F32),
        compiler_params=_cparams(("parallel",)),
        name="proj_in",
    )(x, rows.mod, rows.mod, w_in_b)


def _proj_out_kernel(x_ref, g_ref, yab_ref, yc_ref, wab_ref, wc_ref, o_ref):
    y = (jnp.dot(yab_ref[...], wab_ref[...], preferred_element_type=F32)
         + jnp.dot(yc_ref[...], wc_ref[...], preferred_element_type=F32))
    o_ref[...] = x_ref[...] + g_ref[...] * y


def _proj_out(x, rows, l, yab, yc, w_out_b):
    return pl.pallas_call(
        _proj_out_kernel,
        grid=(rows.n_tiles,),
        in_specs=[rows.row_spec(D_MODEL), rows.mod_spec(5), rows.row_spec(D_AB), rows.row_spec(D_C),
                  pl.BlockSpec((None, D_AB, D_MODEL), lambda i: (l, 0, 0)),
                  pl.BlockSpec((None, D_C, D_MODEL), lambda i: (l, D_AB // D_C, 0))],
        out_specs=rows.row_spec(D_MODEL),
        out_shape=jax.ShapeDtypeStruct((rows.n_rows, D_MODEL), F32),
        compiler_params=_cparams(("parallel",)),
        name="proj_out",
    )(x, rows.mod, yab, yc, w_out_b, w_out_b)


def _retention_log_decay():
    return jnp.log1p(-jnp.exp(jnp.linspace(math.log(1.0 / 32.0), math.log(1.0 / 512.0), N_HEADS, dtype=F32)))


def _rope_tables(pos):
    half = HEAD_DIM // 2
    inv = ROPE_BASE ** (-jnp.arange(half, dtype=F32) / half)
    ang = pos[:, None] * inv[None, :]
    cos, sin = jnp.cos(ang), jnp.sin(ang)
    cos_t = jnp.tile(jnp.concatenate([cos, cos], axis=-1), (1, N_HEADS))
    sin_t = jnp.tile(jnp.concatenate([-sin, sin], axis=-1), (1, N_HEADS))
    return cos_t, sin_t


def _retention_chunk_tables(c):
    lg = _retention_log_decay()
    idx = jnp.arange(c, dtype=F32)
    diff = idx[:, None] - idx[None, :]
    dmask = jnp.where(diff[None] >= 0, jnp.exp(jnp.maximum(diff, 0.0)[None] * lg[:, None, None]), 0.0)
    q_dec = jnp.exp((idx[:, None] + 1.0) * lg[None, :])
    k_dec = jnp.exp((c - 1.0 - idx)[:, None] * lg[None, :])
    s_dec = jnp.exp(c * lg)
    qdec_t = jnp.repeat(q_dec, DV_B, axis=1)
    kdec_t = jnp.repeat(k_dec, HEAD_DIM, axis=1)
    sdec_t = jnp.broadcast_to(jnp.repeat(s_dec, HEAD_DIM)[:, None], (N_HEADS * HEAD_DIM, DV_B))
    return dmask, qdec_t, kdec_t, sdec_t


def _mix_ab_kernel(z_ref, cos_ref, sin_ref, ws_ref, bs_ref, gain_ref, dmask_ref, qdec_ref, kdec_ref,
                   sdec_ref, gng_ref, yab_ref, s_ref):
    @pl.when(pl.program_id(1) == 0)
    def _():
        s_ref[...] = jnp.zeros_like(s_ref)

    gu = _gelu(z_ref[:, 0:D_A])
    gv = _gelu(z_ref[:, D_A:2 * D_A])
    mean_sel = _group_sel(D_A, D_A, HEAD_DIM, HEAD_DIM, 1.0 / HEAD_DIM)
    vn = gv * lax.rsqrt(_dot_sel(gv * gv, mean_sel) + NORM_EPS) * gain_ref[...]
    lane_head = _head_of(vn.shape, 1, HEAD_DIM)
    row = lax.broadcasted_iota(jnp.int32, (CHUNK, CHUNK), 0)
    col = lax.broadcasted_iota(jnp.int32, (CHUNK, CHUNK), 1)
    mixed = bs_ref[...]
    for h in range(N_HEADS):
        w_m = jnp.where(row >= col, ws_ref[h], 0.0)
        mixed = mixed + _dot(w_m, jnp.where(lane_head == h, vn, 0.0))
    yab_ref[:, 0:D_A] = (gu * mixed).astype(BF16)

    q = _rotate_half(z_ref[:, 2 * D_A:3 * D_A], cos_ref[...], sin_ref[...])
    k = _rotate_half(z_ref[:, 3 * D_A:4 * D_A], cos_ref[...], sin_ref[...]) * (HEAD_DIM ** -0.5)
    s_all = s_ref[...]
    kd_t = (k * kdec_ref[...]).T
    qb = q.astype(BF16)
    kb = k.astype(BF16)
    sb = s_all.astype(BF16)
    for h in range(N_HEADS):
        v_h = z_ref[:, 4 * D_A + h * DV_B:4 * D_A + (h + 1) * DV_B]
        zg_h = z_ref[:, 4 * D_A + D_BV + h * DV_B:4 * D_A + D_BV + (h + 1) * DV_B]
        q_h = jnp.where(lane_head == h, qb, jnp.zeros_like(qb))
        scores = _dot_t(q_h, kb) * dmask_ref[h]
        inner = _dot(scores, v_h)
        cross = jnp.dot(q_h, sb, preferred_element_type=F32) * qdec_ref[:, h * DV_B:(h + 1) * DV_B]
        ob = inner + cross
        yb = _rms(ob) * gng_ref[:, h * DV_B:(h + 1) * DV_B] * _silu(zg_h)
        yab_ref[:, D_A + h * DV_B:D_A + (h + 1) * DV_B] = yb.astype(BF16)
        rows_h = slice(h * HEAD_DIM, (h + 1) * HEAD_DIM)
        s_ref[rows_h, :] = s_all[rows_h, :] * sdec_ref[rows_h, :] + _dot(kd_t[rows_h, :], v_h)


def _mix_ab_prompt(z, bsz, t, l, p):
    n_c = t // CHUNK
    pos = jnp.arange(t, dtype=F32)
    cos_t, sin_t = _rope_tables(pos)
    dmask, qdec_t, kdec_t, sdec_t = _retention_chunk_tables(CHUNK)
    bs_t = jnp.repeat(p['sg_b_s'][l].T, HEAD_DIM, axis=1)
    gain = p['sg_v_gain'][l].reshape(1, D_A)
    gng = p['ret_gn_g'][l].reshape(1, D_BV)
    const = lambda shape: pl.BlockSpec(shape, lambda b, c: (0,) * len(shape))
    yab, s_ret = pl.pallas_call(
        _mix_ab_kernel,
        grid=(bsz, n_c),
        in_specs=[pl.BlockSpec((CHUNK, Z_AB), lambda b, c: (b * n_c + c, 0)),
                  pl.BlockSpec((CHUNK, D_A), lambda b, c: (c, 0)),
                  pl.BlockSpec((CHUNK, D_A), lambda b, c: (c, 0)),
                  pl.BlockSpec((None, N_HEADS, CHUNK, CHUNK), lambda b, c: (l, 0, 0, 0)),
                  const((CHUNK, D_A)), const((1, D_A)), const((N_HEADS, CHUNK, CHUNK)),
                  const((CHUNK, D_BV)), const((CHUNK, D_A)), const((D_A, DV_B)), const((1, D_BV))],
        out_specs=[pl.BlockSpec((CHUNK, D_AB), lambda b, c: (b * n_c + c, 0)),
                   pl.BlockSpec((None, D_A, DV_B), lambda b, c: (b, 0, 0))],
        out_shape=[jax.ShapeDtypeStruct((bsz * t, D_AB), BF16),
                   jax.ShapeDtypeStruct((bsz, D_A, DV_B), F32)],
        compiler_params=_cparams(("parallel", "arbitrary")),
        name="mix_ab",
    )(z, cos_t, sin_t, p['sg_w_s'], bs_t, gain, dmask, qdec_t, kdec_t, sdec_t, gng)
    return yab, s_ret.reshape(bsz, N_HEADS, HEAD_DIM, DV_B)


def _softplus(x):
    return jnp.maximum(x, 0.0) + jnp.log(1.0 + jnp.exp(-jnp.abs(x)))


def _rwkv_vectors(zc, shifted, mu, w0, w2p, a0, a2p, g2, k_k, k_a):
    zs = zc + (shifted - zc) * mu
    r = zs[:, 0:D_C]
    k = zs[:, D_C:2 * D_C]
    v = zs[:, 2 * D_C:3 * D_C]
    xwa = zs[:, 3 * D_C:3 * D_C + 128]
    xg = zs[:, 3 * D_C + 128:C_IN]
    w_log = -_softplus(-(w0 + _dot(jnp.tanh(xwa), w2p))) - 0.5
    logw = -jnp.exp(w_log)
    a = jax.nn.sigmoid(a0 + _dot(xwa, a2p))
    g = _dot(jax.nn.sigmoid(xg), g2)
    kk = k * k_k
    sum_sel = _group_sel(D_C, D_C, HEAD_DIM, HEAD_DIM)
    kk = kk / jnp.maximum(jnp.sqrt(_dot_sel(kk * kk, sum_sel)), 1e-12)
    k2 = k * (1.0 + (a - 1.0) * k_a)
    return r, k2, v, logw, a, g, kk


def _rwkv_output(y, r, k2, v, g, r_k, ln_g, ln_b):
    mean_sel = _group_sel(D_C, D_C, HEAD_DIM, HEAD_DIM, 1.0 / HEAD_DIM)
    sum_sel = _group_sel(D_C, D_C, HEAD_DIM, HEAD_DIM)
    yc = y - _dot_sel(y, mean_sel)
    var = _dot_sel(yc * yc, mean_sel)
    yn = yc * lax.rsqrt(var + GN_EPS) * ln_g + ln_b
    bonus = _dot_sel(r * k2 * r_k, sum_sel) * v
    return (yn + bonus) * g


def _rwkv_kernel(zc_ref, mu_ref, w0_ref, w2p_ref, a0_ref, a2p_ref, g2_ref, kk_ref, ka_ref, rk_ref,
                 lng_ref, lnb_ref, yc_ref, s_ref, prev_ref):
    c_len = RW_CHUNK

    @pl.when(pl.program_id(1) == 0)
    def _():
        s_ref[...] = jnp.zeros_like(s_ref)
        prev_ref[...] = jnp.zeros_like(prev_ref)

    zc = zc_ref[...]
    row_id = lax.broadcasted_iota(jnp.int32, zc.shape, 0)
    shifted = jnp.where(row_id == 0, prev_ref[...], pltpu.roll(zc, 1, axis=0))
    prev_ref[...] = zc[c_len - 1:c_len, :]
    r, k2, v, logw, a, g, kk = _rwkv_vectors(zc, shifted, mu_ref[...], w0_ref[...], w2p_ref[...],
                                             a0_ref[...], a2p_ref[...], g2_ref[...], kk_ref[...],
                                             ka_ref[...])
    b = kk * a

    tri = jnp.where(lax.broadcasted_iota(jnp.int32, (c_len, c_len), 0)
                    >= lax.broadcasted_iota(jnp.int32, (c_len, c_len), 1), 1.0, 0.0).astype(BF16)
    hi, mid, lo = _split3(logw)
    cum = (jnp.dot(tri, hi, preferred_element_type=F32) + jnp.dot(tri, mid, preferred_element_type=F32)
           + jnp.dot(tri, lo, preferred_element_type=F32))
    c_last = cum[c_len - 1:c_len, :]
    e_inv = jnp.exp(-cum)
    a_hat = -kk * jnp.exp(cum - logw)
    r_hat = r * jnp.exp(cum)
    e_tail = jnp.exp(c_last - cum)

    s_bd = s_ref[...]
    bk_t = jnp.concatenate([_stack_heads(b * e_inv, HEAD_DIM), _stack_heads(k2 * e_inv, HEAD_DIM)], axis=0)
    gram = _dot3_t(jnp.concatenate([a_hat, r_hat], axis=0), bk_t)
    t_id = lax.broadcasted_iota(jnp.int32, (c_len, N_HEADS * c_len), 0)
    s_id = lax.broadcasted_iota(jnp.int32, (c_len, N_HEADS * c_len), 1) % c_len
    strict = s_id < t_id
    incl = s_id <= t_id
    nw = N_HEADS * c_len
    x_w = jnp.where(strict, gram[0:c_len, 0:nw], 0.0)
    l_ak = jnp.where(strict, gram[0:c_len, nw:2 * nw], 0.0)
    m_rb = jnp.where(incl, gram[c_len:2 * c_len, 0:nw], 0.0)
    m_rk = jnp.where(incl, gram[c_len:2 * c_len, nw:2 * nw], 0.0)

    v_bd = _stack_heads(v, HEAD_DIM)
    w_rhs = _dot3_t(a_hat, s_bd) + _dot3(l_ak, v_bd)

    bd = _head_of((nw, nw), 0, c_len) == _head_of((nw, nw), 1, c_len)
    powers = [jnp.where(bd, jnp.concatenate([x_w] * N_HEADS, axis=0), 0.0)]
    for _ in range(5):
        powers.append(_dot3(powers[-1], powers[-1]))
    u_bd = _stack_heads(w_rhs, HEAD_DIM)
    for p_mat in reversed(powers):
        u_bd = u_bd + _dot3(p_mat, u_bd)
    u = u_bd[0:c_len] + u_bd[c_len:2 * c_len] + u_bd[2 * c_len:3 * c_len] + u_bd[3 * c_len:4 * c_len]

    y = _dot3_t(r_hat, s_bd) + _dot3(m_rb, u_bd) + _dot3(m_rk, v_bd)
    out = _rwkv_output(y, r, k2, v, g, rk_ref[...], lng_ref[...], lnb_ref[...])
    yc_ref[...] = out.astype(BF16)

    uv_t = jnp.concatenate([u, v], axis=0).T
    s_upd = _dot3(uv_t, jnp.concatenate([b * e_tail, k2 * e_tail], axis=0))
    bd_s = _head_of(s_bd.shape, 0, HEAD_DIM) == _head_of(s_bd.shape, 1, HEAD_DIM)
    s_ref[...] = s_bd * jnp.exp(c_last) + jnp.where(bd_s, s_upd, 0.0)


def _rwkv_params(l, p):
    zeros = jnp.zeros((HEAD_DIM, D_C), F32)
    w2p = jnp.concatenate([p['rw_w2'][l], zeros], axis=0).astype(BF16)
    a2p = jnp.concatenate([zeros, p['rw_a2'][l]], axis=0).astype(BF16)
    row = lambda a: a.reshape(1, -1)
    return [row(p['rw_mu'][l]), row(p['rw_w0'][l]), w2p, row(p['rw_a0'][l]), a2p,
            p['rw_g2'][l].astype(BF16), row(p['rw_k_k'][l]), row(p['rw_k_a'][l])]


def _rwkv_prompt(z, bsz, t, l, p):
    n_c = t // RW_CHUNK
    params = _rwkv_params(l, p) + [p['rw_r_k'][l].reshape(1, D_C), p['rw_ln_g'][l].reshape(1, D_C),
                                   p['rw_ln_b'][l].reshape(1, D_C)]
    const = lambda a: pl.BlockSpec(a.shape, lambda b, c: (0,) * a.ndim)
    yc, s_bd = pl.pallas_call(
        _rwkv_kernel,
        grid=(bsz, n_c),
        in_specs=[pl.BlockSpec((RW_CHUNK, C_IN), lambda b, c: (b * n_c + c, Z_AB // C_IN))]
                 + [const(a) for a in params],
        out_specs=[pl.BlockSpec((RW_CHUNK, D_C), lambda b, c: (b * n_c + c, 0)),
                   pl.BlockSpec((None, D_C, D_C), lambda b, c: (b, 0, 0))],
        out_shape=[jax.ShapeDtypeStruct((bsz * t, D_C), BF16),
                   jax.ShapeDtypeStruct((bsz, D_C, D_C), F32)],
        scratch_shapes=[pltpu.VMEM((1, C_IN), F32)],
        compiler_params=_cparams(("parallel", "arbitrary")),
        name="rwkv_chunk",
    )(z, *params)
    s_heads = jnp.stack([s_bd[:, h * HEAD_DIM:(h + 1) * HEAD_DIM, h * HEAD_DIM:(h + 1) * HEAD_DIM]
                         for h in range(N_HEADS)], axis=1)
    return yc, s_heads


SAMPLE_SLOTS = 11


def _samp_prep_kernel(z_ref, prev_ref, cos_ref, sin_ref, ws0_ref, bs0_ref, gain_ref, mu_ref, w0_ref,
                      w2p_ref, a0_ref, a2p_ref, g2_ref, kk_ref, ka_ref, vec_ref, qk_ref):
    gu = _gelu(z_ref[:, 0:D_A])
    gv = _gelu(z_ref[:, D_A:2 * D_A])
    mean_sel = _group_sel(D_A, D_A, HEAD_DIM, HEAD_DIM, 1.0 / HEAD_DIM)
    vn = gv * lax.rsqrt(_dot_sel(gv * gv, mean_sel) + NORM_EPS) * gain_ref[...]
    vec_ref[0] = gu * (ws0_ref[...] * vn + bs0_ref[...])
    vec_ref[1] = vn
    q = _rotate_half(z_ref[:, 2 * D_A:3 * D_A], cos_ref[...], sin_ref[...])
    k = _rotate_half(z_ref[:, 3 * D_A:4 * D_A], cos_ref[...], sin_ref[...]) * (HEAD_DIM ** -0.5)
    vec_ref[2] = q
    vec_ref[3] = k
    qk_ref[...] = _dot_sel(q * k, _group_sel(D_A, D_BV, HEAD_DIM, DV_B))
    r, k2, v, logw, a, g, kk = _rwkv_vectors(z_ref[:, Z_AB:D_IN], prev_ref[...], mu_ref[...], w0_ref[...],
                                             w2p_ref[...], a0_ref[...], a2p_ref[...], g2_ref[...],
                                             kk_ref[...], ka_ref[...])
    vec_ref[4] = r
    vec_ref[5] = jnp.exp(logw)
    vec_ref[6] = k2
    vec_ref[7] = kk
    vec_ref[8] = kk * a
    vec_ref[9] = v
    vec_ref[10] = g


def _samp_state_kernel(sret_ref, swkv_ref, qc_ref, kc_ref, vr_ref, sdec_ref, w_ref, kk_ref, b_ref,
                       k2_ref, r_ref, vc_ref, sret_o_ref, cross_ref, swkv_o_ref, y_ref):
    s = sret_ref[...]
    cross_ref[...] = jnp.sum(s * qc_ref[...], axis=2, keepdims=True)
    sret_o_ref[...] = s * sdec_ref[...][None] + kc_ref[...] * vr_ref[...]
    sw = swkv_ref[...]
    sa = jnp.sum(sw * (-kk_ref[...]), axis=3, keepdims=True)
    sw = sw * w_ref[...] + sa * b_ref[...] + vc_ref[...] * k2_ref[...]
    swkv_o_ref[...] = sw
    y_ref[...] = jnp.sum(sw * r_ref[...], axis=3, keepdims=True)


def _samp_post_kernel(vec_ref, qk_ref, cross_ref, y_ref, zv_ref, zg_ref, qdec_ref, gng_ref, rk_ref,
                      lng_ref, lnb_ref, yab_ref, yc_ref):
    yab_ref[:, 0:D_A] = vec_ref[0].astype(BF16)
    ob = qk_ref[...] * zv_ref[...] + cross_ref[...] * qdec_ref[...]
    for h in range(N_HEADS):
        cols = slice(h * DV_B, (h + 1) * DV_B)
        yb = _rms(ob[:, cols]) * gng_ref[:, cols] * _silu(zg_ref[:, cols])
        yab_ref[:, D_A + h * DV_B:D_A + (h + 1) * DV_B] = yb.astype(BF16)
    out = _rwkv_output(y_ref[...], vec_ref[4], vec_ref[6], vec_ref[9], vec_ref[10], rk_ref[...],
                       lng_ref[...], lnb_ref[...])
    yc_ref[...] = out.astype(BF16)


def _mix_sample(z, s_ret, s_wkv, shift, l, p):
    bs = z.shape[0]
    bb = 8
    full = lambda a: pl.BlockSpec(a.shape, lambda *_: (0,) * a.ndim)
    cos_t, sin_t = _rope_tables(jnp.full((1,), PAST_LEN, F32))
    lg = _retention_log_decay()
    ws0 = jnp.repeat(p['sg_w_s'][l][:, 0, 0], HEAD_DIM).reshape(1, D_A)
    bs0 = jnp.repeat(p['sg_b_s'][l][:, 0], HEAD_DIM).reshape(1, D_A)
    gain = p['sg_v_gain'][l].reshape(1, D_A)
    prep_args = [z, shift, cos_t, sin_t, ws0, bs0, gain] + _rwkv_params(l, p)
    vec, qk = pl.pallas_call(
        _samp_prep_kernel,
        grid=(1,),
        in_specs=[full(a) for a in prep_args],
        out_specs=[pl.BlockSpec((SAMPLE_SLOTS, bs, D_A), lambda i: (0, 0, 0)),
                   pl.BlockSpec((bs, D_BV), lambda i: (0, 0))],
        out_shape=[jax.ShapeDtypeStruct((SAMPLE_SLOTS, bs, D_A), F32),
                   jax.ShapeDtypeStruct((bs, D_BV), F32)],
        compiler_params=_cparams(("arbitrary",)),
        name="sample_prep",
    )(*prep_args)

    zv = z[:, 4 * D_A:4 * D_A + D_BV]
    zg = z[:, 4 * D_A + D_BV:Z_AB]
    col = lambda a: a.reshape(bs, N_HEADS, HEAD_DIM, 1)
    rowv = lambda a: a.reshape(bs, N_HEADS, 1, HEAD_DIM)
    sdec_t = jnp.broadcast_to(jnp.exp(lg)[:, None, None], (N_HEADS, HEAD_DIM, DV_B))
    state_args = [s_ret, s_wkv, col(vec[2]), col(vec[3]), zv.reshape(bs, N_HEADS, 1, DV_B), sdec_t,
                  rowv(vec[5]), rowv(vec[7]), rowv(vec[8]), rowv(vec[6]), rowv(vec[4]), col(vec[9])]
    blk = lambda a: pl.BlockSpec((bb,) + a.shape[1:], lambda i: (i, 0, 0, 0))
    state_specs = [blk(a) for a in state_args]
    state_specs[5] = pl.BlockSpec(sdec_t.shape, lambda i: (0, 0, 0))
    out_shapes = [jax.ShapeDtypeStruct((bs, N_HEADS, HEAD_DIM, DV_B), F32),
                  jax.ShapeDtypeStruct((bs, N_HEADS, 1, DV_B), F32),
                  jax.ShapeDtypeStruct((bs, N_HEADS, HEAD_DIM, HEAD_DIM), F32),
                  jax.ShapeDtypeStruct((bs, N_HEADS, HEAD_DIM, 1), F32)]
    s_ret_new, cross, s_wkv_new, y = pl.pallas_call(
        _samp_state_kernel,
        grid=(bs // bb,),
        in_specs=state_specs,
        out_specs=[blk(a) for a in out_shapes],
        out_shape=out_shapes,
        compiler_params=_cparams(("parallel",)),
        name="sample_state",
    )(*state_args)

    qdec_t = jnp.repeat(jnp.exp(lg), DV_B).reshape(1, D_BV)
    post_args = [vec, qk, cross.reshape(bs, D_BV), y.reshape(bs, D_C), zv, zg, qdec_t,
                 p['ret_gn_g'][l].reshape(1, D_BV), p['rw_r_k'][l].reshape(1, D_C),
                 p['rw_ln_g'][l].reshape(1, D_C), p['rw_ln_b'][l].reshape(1, D_C)]
    yab, yc = pl.pallas_call(
        _samp_post_kernel,
        grid=(1,),
        in_specs=[full(a) for a in post_args],
        out_specs=[pl.BlockSpec((bs, D_AB), lambda i: (0, 0)), pl.BlockSpec((bs, D_C), lambda i: (0, 0))],
        out_shape=[jax.ShapeDtypeStruct((bs, D_AB), BF16), jax.ShapeDtypeStruct((bs, D_C), BF16)],
        compiler_params=_cparams(("arbitrary",)),
        name="sample_post",
    )(*post_args)
    return yab, yc, s_ret_new, s_wkv_new, vec[1]


def _row_tile(n_rows_per_seq):
    for t in (512, 256, 128):
        if n_rows_per_seq % t == 0:
            return t
    raise ValueError("sequence length must be a multiple of 128")


def kernel(x_prompt, x_sample, state_ret, state_wkv, state_shift, c_prompt, c_sample,
           w_ada, b_ada, w_ffn1_in, w_ffn1_out, w_in, w_out, w_ffn2_in, w_ffn2_out,
           sg_v_gain, sg_w_s, sg_b_s, ret_gn_g,
           rw_mu, rw_w0, rw_w2, rw_a0, rw_a2, rw_g2, rw_k_k, rw_k_a, rw_r_k, rw_ln_g, rw_ln_b,
           final_g):
    p = dict(sg_v_gain=sg_v_gain, sg_w_s=sg_w_s, sg_b_s=sg_b_s, ret_gn_g=ret_gn_g, rw_mu=rw_mu,
             rw_w0=rw_w0, rw_w2=rw_w2, rw_a0=rw_a0, rw_a2=rw_a2, rw_g2=rw_g2, rw_k_k=rw_k_k,
             rw_k_a=rw_k_a, rw_r_k=rw_r_k, rw_ln_g=rw_ln_g, rw_ln_b=rw_ln_b, final_g=final_g)
    wb = dict(ffn1_in=w_ffn1_in.astype(BF16), ffn1_out=w_ffn1_out.astype(BF16), w_in=w_in.astype(BF16),
              w_out=w_out.astype(BF16), ffn2_in=w_ffn2_in.astype(BF16), ffn2_out=w_ffn2_out.astype(BF16))
    bp, t, _ = x_prompt.shape
    bs = x_sample.shape[0]
    assert x_sample.shape[1] == 1 and t % CHUNK == 0

    mod = _ada(jnp.concatenate([c_prompt, c_sample], axis=0), w_ada, b_ada)
    mod = mod.reshape(DEPTH, bp + bs, N_MOD, D_MODEL)

    def prompt_mixer(z, l):
        yab, s_ret = _mix_ab_prompt(z, bp, t, l, p)
        yc, s_wkv = _rwkv_prompt(z, bp, t, l, p)
        last = z.reshape(bp, t, D_IN)[:, t - 1, Z_AB:]
        return yab, yc, (s_ret, s_wkv, last)

    tile_p = _row_tile(t)
    def run(x, make_rows, mixer):
        extras = []
        for l in range(DEPTH):
            rows = make_rows(l)
            x = _ffn(x, rows, l, 0, wb['ffn1_in'], wb['ffn1_out'])
            z = _proj_in(x, rows, l, wb['w_in'])
            yab, yc, extra = mixer(z, l)
            x = _proj_out(x, rows, l, yab, yc, wb['w_out'])
            x = _ffn(x, rows, l, 6, wb['ffn2_in'], wb['ffn2_out'],
                     final_g=final_g if l == DEPTH - 1 else None)
            extras.append(extra)
        return x, extras

    rows_p = lambda l: _Rows(mod[l, :bp].reshape(bp, N_MOD, 1, D_MODEL), bp * t, tile_p, t, False)
    y_p, ex_p = run(x_prompt.reshape(bp * t, D_MODEL), rows_p, prompt_mixer)

    def sample_mixer(z, l):
        yab, yc, s_ret, s_wkv, vn = _mix_sample(z, state_ret[l], state_wkv[l], state_shift[l], l, p)
        return yab, yc, (s_ret, s_wkv, z[:, Z_AB:], vn)

    rows_s = lambda l: _Rows(jnp.swapaxes(mod[l, bp:], 0, 1), bs, bs, 1, True)
    y_s, ex_s = run(x_sample.reshape(bs, D_MODEL), rows_s, sample_mixer)

    stack = lambda ex, i: jnp.stack([e[i] for e in ex])
    return (y_p.reshape(bp, t, D_MODEL), y_s.reshape(bs, 1, D_MODEL),
            stack(ex_p, 0), stack(ex_p, 1), stack(ex_p, 2),
            stack(ex_s, 0), stack(ex_s, 1), stack(ex_s, 2),
            stack(ex_s, 3).reshape(DEPTH, bs, 1, D_A))
```

```python
import functools
import math

import jax
import jax.numpy as jnp
from jax import lax
from jax.experimental import pallas as pl
from jax.experimental.pallas import tpu as pltpu

F32 = jnp.float32
BF16 = jnp.bfloat16

D_MODEL = 1024
DEPTH = 4
N_MOD = 9
D_FF = 2816
N_HEADS = 4
HEAD_DIM = 64
DV_B = 128
CHUNK = 128
RW_CHUNK = 64
RW_SEQS_PER_STEP = 8
D_A = N_HEADS * HEAD_DIM
D_BV = N_HEADS * DV_B
D_C = N_HEADS * HEAD_DIM
C_IN = 1024
D_IN = 3072
Z_AB = 2048
D_AB = D_A + D_BV
ROPE_BASE = 10000.0
NORM_EPS = 1e-6
GN_EPS = 64e-5
PAST_LEN = 16384

VMEM_LIMIT_BYTES = 56 * 1024 * 1024
FFN_TILE_F = 1408


def _cparams(sem):
    return pltpu.CompilerParams(dimension_semantics=sem, vmem_limit_bytes=VMEM_LIMIT_BYTES)


def _dot(a, b):
    return jnp.dot(a.astype(BF16), b.astype(BF16), preferred_element_type=F32)


def _dot_t(a, b):
    return lax.dot_general(a.astype(BF16), b.astype(BF16), (((1,), (1,)), ((), ())),
                           preferred_element_type=F32)


def _split2(x):
    hi = x.astype(BF16)
    lo = (x - hi.astype(F32)).astype(BF16)
    return hi, lo


def _split3(x):
    hi = x.astype(BF16)
    r1 = x - hi.astype(F32)
    mid = r1.astype(BF16)
    lo = (r1 - mid.astype(F32)).astype(BF16)
    return hi, mid, lo


def _dot_sel(x, sel):
    hi, mid, lo = _split3(x)
    return (jnp.dot(hi, sel, preferred_element_type=F32)
            + jnp.dot(mid, sel, preferred_element_type=F32)
            + jnp.dot(lo, sel, preferred_element_type=F32))


def _dot3(a, b):
    ah, al = _split2(a)
    bh, bl = _split2(b)
    return (jnp.dot(ah, bh, preferred_element_type=F32)
            + jnp.dot(ah, bl, preferred_element_type=F32)
            + jnp.dot(al, bh, preferred_element_type=F32))


def _dot3_t(a, b):
    ah, al = _split2(a)
    bh, bl = _split2(b)
    dn = (((1,), (1,)), ((), ()))
    return (lax.dot_general(ah, bh, dn, preferred_element_type=F32)
            + lax.dot_general(ah, bl, dn, preferred_element_type=F32)
            + lax.dot_general(al, bh, dn, preferred_element_type=F32))


def _rms(x, eps=NORM_EPS):
    return x * lax.rsqrt(jnp.mean(x * x, axis=-1, keepdims=True) + eps)


def _silu(x):
    return x * jax.nn.sigmoid(x)


def _gelu(x):
    return jax.nn.gelu(x)


def _head_of(shape, dim, width):
    return lax.broadcasted_iota(jnp.int32, shape, dim) // width


def _group_sel(n_in, n_out, w_in, w_out, scale=1.0):
    gi = _head_of((n_in, n_out), 0, w_in)
    go = _head_of((n_in, n_out), 1, w_out)
    return jnp.where(gi == go, scale, 0.0).astype(BF16)


def _stack_heads(x, width):
    lane_head = _head_of(x.shape, 1, width)
    return jnp.concatenate([jnp.where(lane_head == h, x, 0.0) for h in range(N_HEADS)], axis=0)


def _rotate_half(x, cos_t, sin_t):
    n = x.shape[1]
    half = HEAD_DIM // 2
    lane = lax.broadcasted_iota(jnp.int32, x.shape, 1)
    fwd = pltpu.roll(x, n - half, axis=1)
    bwd = pltpu.roll(x, half, axis=1)
    partner = jnp.where((lane % HEAD_DIM) < half, fwd, bwd)
    return x * cos_t + partner * sin_t


def _ada_kernel(c_ref, w_ref, b_ref, o_ref):
    c = c_ref[...]
    o_ref[...] = _dot(_silu(c), w_ref[...]) + b_ref[...]


def _ada(c_all, w_ada, b_ada):
    bt = c_all.shape[0]
    n_out = N_MOD * D_MODEL
    tn = 1152
    return pl.pallas_call(
        _ada_kernel,
        grid=(DEPTH, n_out // tn),
        in_specs=[pl.BlockSpec((bt, D_MODEL), lambda l, j: (0, 0)),
                  pl.BlockSpec((None, D_MODEL, tn), lambda l, j: (l, 0, j)),
                  pl.BlockSpec((None, 1, tn), lambda l, j: (l, 0, j))],
        out_specs=pl.BlockSpec((None, bt, tn), lambda l, j: (l, 0, j)),
        out_shape=jax.ShapeDtypeStruct((DEPTH, bt, n_out), F32),
        compiler_params=_cparams(("parallel", "parallel")),
        name="ada_mod",
    )(c_all, w_ada, b_ada.reshape(DEPTH, 1, n_out))


class _Rows:
    def __init__(self, mod, n_rows, tile, rows_per_seq, per_row):
        self.mod, self.n_rows, self.tile, self.per_row = mod, n_rows, tile, per_row
        self.n_tiles = n_rows // tile
        self.tiles_per_seq = None if per_row else rows_per_seq // tile

    def mod_spec(self, j):
        if self.per_row:
            return pl.BlockSpec((None, self.tile, D_MODEL), lambda i, *_: (j, i, 0))
        tps = self.tiles_per_seq
        return pl.BlockSpec((None, None, 1, D_MODEL), lambda i, *_: (i // tps, j, 0, 0))

    def row_spec(self, width):
        return pl.BlockSpec((self.tile, width), lambda i, *_: (i, 0))


def _ffn_kernel(*refs, n_f, final):
    if final:
        x_ref, sh_ref, sc_ref, g_ref, wg_ref, wu_ref, wo_ref, fg_ref, o_ref, h_ref, acc_ref = refs
    else:
        x_ref, sh_ref, sc_ref, g_ref, wg_ref, wu_ref, wo_ref, o_ref, h_ref, acc_ref = refs
    j = pl.program_id(1)

    @pl.when(j == 0)
    def _():
        h_ref[...] = (_rms(x_ref[...]) * (1.0 + sc_ref[...]) + sh_ref[...]).astype(BF16)

    hb = h_ref[...]
    gate = jnp.dot(hb, wg_ref[...], preferred_element_type=F32)
    up = jnp.dot(hb, wu_ref[...], preferred_element_type=F32)
    y = jnp.dot((_silu(gate) * up).astype(BF16), wo_ref[...], preferred_element_type=F32)

    @pl.when(j == 0)
    def _():
        acc_ref[...] = y

    @pl.when(jnp.logical_and(j > 0, j < n_f - 1))
    def _():
        acc_ref[...] += y

    @pl.when(j == n_f - 1)
    def _():
        out = x_ref[...] + 0.5 * g_ref[...] * (acc_ref[...] + y)
        if final:
            out = _rms(out) * fg_ref[...]
        o_ref[...] = out


def _ffn(x, rows, l, jm, w_in_b, w_out_b, final_g=None):
    n_f = D_FF // FFN_TILE_F
    tf = FFN_TILE_F
    final = final_g is not None
    in_specs = [rows.row_spec(D_MODEL), rows.mod_spec(jm), rows.mod_spec(jm + 1), rows.mod_spec(jm + 2),
                pl.BlockSpec((None, D_MODEL, tf), lambda i, j: (l, 0, j)),
                pl.BlockSpec((None, D_MODEL, tf), lambda i, j: (l, 0, n_f + j)),
                pl.BlockSpec((None, tf, D_MODEL), lambda i, j: (l, j, 0))]
    args = [x, rows.mod, rows.mod, rows.mod, w_in_b, w_in_b, w_out_b]
    if final:
        in_specs.append(pl.BlockSpec((1, D_MODEL), lambda i, j: (0, 0)))
        args.append(final_g.reshape(1, D_MODEL))
    return pl.pallas_call(
        functools.partial(_ffn_kernel, n_f=n_f, final=final),
        grid=(rows.n_tiles, n_f),
        in_specs=in_specs,
        out_specs=rows.row_spec(D_MODEL),
        out_shape=jax.ShapeDtypeStruct((rows.n_rows, D_MODEL), F32),
        scratch_shapes=[pltpu.VMEM((rows.tile, D_MODEL), BF16), pltpu.VMEM((rows.tile, D_MODEL), F32)],
        compiler_params=_cparams(("parallel", "arbitrary")),
        name="ffn_final" if final else "ffn",
    )(*args)


def _proj_in_kernel(x_ref, sh_ref, sc_ref, w_ref, z_ref):
    h = (_rms(x_ref[...]) * (1.0 + sc_ref[...]) + sh_ref[...]).astype(BF16)
    z_ref[...] = jnp.dot(h, w_ref[...], preferred_element_type=F32)


def _proj_in(x, rows, l, w_in_b):
    return pl.pallas_call(
        _proj_in_kernel,
        grid=(rows.n_tiles,),
        in_specs=[rows.row_spec(D_MODEL), rows.mod_spec(3), rows.mod_spec(4),
                  pl.BlockSpec((None, D_MODEL, D_IN), lambda i: (l, 0, 0))],
        out_specs=rows.row_spec(D_IN),
        out_shape=jax.ShapeDtypeStruct((rows.n_rows, D_IN), F32),
        compiler_params=_cparams(("parallel",)),
        name="proj_in",
    )(x, rows.mod, rows.mod, w_in_b)


def _proj_out_kernel(x_ref, g_ref, yab_ref, yc_ref, wab_ref, wc_ref, o_ref):
    y = (jnp.dot(yab_ref[...], wab_ref[...], preferred_element_type=F32)
         + jnp.dot(yc_ref[...], wc_ref[...], preferred_element_type=F32))
    o_ref[...] = x_ref[...] + g_ref[...] * y


def _proj_out(x, rows, l, yab, yc, w_out_b):
    return pl.pallas_call(
        _proj_out_kernel,
        grid=(rows.n_tiles,),
        in_specs=[rows.row_spec(D_MODEL), rows.mod_spec(5), rows.row_spec(D_AB), rows.row_spec(D_C),
                  pl.BlockSpec((None, D_AB, D_MODEL), lambda i: (l, 0, 0)),
                  pl.BlockSpec((None, D_C, D_MODEL), lambda i: (l, D_AB // D_C, 0))],
        out_specs=rows.row_spec(D_MODEL),
        out_shape=jax.ShapeDtypeStruct((rows.n_rows, D_MODEL), F32),
        compiler_params=_cparams(("parallel",)),
        name="proj_out",
    )(x, rows.mod, yab, yc, w_out_b, w_out_b)


def _retention_log_decay():
    return jnp.log1p(-jnp.exp(jnp.linspace(math.log(1.0 / 32.0), math.log(1.0 / 512.0), N_HEADS, dtype=F32)))


def _rope_tables(pos):
    half = HEAD_DIM // 2
    inv = ROPE_BASE ** (-jnp.arange(half, dtype=F32) / half)
    ang = pos[:, None] * inv[None, :]
    cos, sin = jnp.cos(ang), jnp.sin(ang)
    cos_t = jnp.tile(jnp.concatenate([cos, cos], axis=-1), (1, N_HEADS))
    sin_t = jnp.tile(jnp.concatenate([-sin, sin], axis=-1), (1, N_HEADS))
    return cos_t, sin_t


def _retention_chunk_tables(c):
    lg = _retention_log_decay()
    idx = jnp.arange(c, dtype=F32)
    diff = idx[:, None] - idx[None, :]
    dmask = jnp.where(diff[None] >= 0, jnp.exp(jnp.maximum(diff, 0.0)[None] * lg[:, None, None]), 0.0)
    q_dec = jnp.exp((idx[:, None] + 1.0) * lg[None, :])
    k_dec = jnp.exp((c - 1.0 - idx)[:, None] * lg[None, :])
    s_dec = jnp.exp(c * lg)
    qdec_t = jnp.repeat(q_dec, DV_B, axis=1)
    kdec_t = jnp.repeat(k_dec, HEAD_DIM, axis=1)
    sdec_t = jnp.broadcast_to(jnp.repeat(s_dec, HEAD_DIM)[:, None], (N_HEADS * HEAD_DIM, DV_B))
    return dmask, qdec_t, kdec_t, sdec_t


def _mix_ab_kernel(z_ref, cos_ref, sin_ref, ws_ref, bs_ref, gain_ref, dmask_ref, qdec_ref, kdec_ref,
                   sdec_ref, gng_ref, yab_ref, s_ref):
    @pl.when(pl.program_id(1) == 0)
    def _():
        s_ref[...] = jnp.zeros_like(s_ref)

    gu = _gelu(z_ref[:, 0:D_A])
    gv = _gelu(z_ref[:, D_A:2 * D_A])
    mean_sel = _group_sel(D_A, D_A, HEAD_DIM, HEAD_DIM, 1.0 / HEAD_DIM)
    vn = gv * lax.rsqrt(_dot_sel(gv * gv, mean_sel) + NORM_EPS) * gain_ref[...]
    lane_head = _head_of(vn.shape, 1, HEAD_DIM)
    row = lax.broadcasted_iota(jnp.int32, (CHUNK, CHUNK), 0)
    col = lax.broadcasted_iota(jnp.int32, (CHUNK, CHUNK), 1)
    mixed = bs_ref[...]
    for h in range(N_HEADS):
        w_m = jnp.where(row >= col, ws_ref[h], 0.0)
        mixed = mixed + _dot(w_m, jnp.where(lane_head == h, vn, 0.0))
    yab_ref[:, 0:D_A] = (gu * mixed).astype(BF16)

    q = _rotate_half(z_ref[:, 2 * D_A:3 * D_A], cos_ref[...], sin_ref[...])
    k = _rotate_half(z_ref[:, 3 * D_A:4 * D_A], cos_ref[...], sin_ref[...]) * (HEAD_DIM ** -0.5)
    s_all = s_ref[...]
    kd_t = (k * kdec_ref[...]).T
    qb = q.astype(BF16)
    kb = k.astype(BF16)
    sb = s_all.astype(BF16)
    for h in range(N_HEADS):
        v_h = z_ref[:, 4 * D_A + h * DV_B:4 * D_A + (h + 1) * DV_B]
        zg_h = z_ref[:, 4 * D_A + D_BV + h * DV_B:4 * D_A + D_BV + (h + 1) * DV_B]
        q_h = jnp.where(lane_head == h, qb, jnp.zeros_like(qb))
        scores = _dot_t(q_h, kb) * dmask_ref[h]
        inner = _dot(scores, v_h)
        cross = jnp.dot(q_h, sb, preferred_element_type=F32) * qdec_ref[:, h * DV_B:(h + 1) * DV_B]
        ob = inner + cross
        yb = _rms(ob) * gng_ref[:, h * DV_B:(h + 1) * DV_B] * _silu(zg_h)
        yab_ref[:, D_A + h * DV_B:D_A + (h + 1) * DV_B] = yb.astype(BF16)
        rows_h = slice(h * HEAD_DIM, (h + 1) * HEAD_DIM)
        s_ref[rows_h, :] = s_all[rows_h, :] * sdec_ref[rows_h, :] + _dot(kd_t[rows_h, :], v_h)


def _mix_ab_prompt(z, bsz, t, l, p):
    n_c = t // CHUNK
    pos = jnp.arange(t, dtype=F32)
    cos_t, sin_t = _rope_tables(pos)
    dmask, qdec_t, kdec_t, sdec_t = _retention_chunk_tables(CHUNK)
    bs_t = jnp.repeat(p['sg_b_s'][l].T, HEAD_DIM, axis=1)
    gain = p['sg_v_gain'][l].reshape(1, D_A)
    gng = p['ret_gn_g'][l].reshape(1, D_BV)
    const = lambda shape: pl.BlockSpec(shape, lambda b, c: (0,) * len(shape))
    yab, s_ret = pl.pallas_call(
        _mix_ab_kernel,
        grid=(bsz, n_c),
        in_specs=[pl.BlockSpec((CHUNK, Z_AB), lambda b, c: (b * n_c + c, 0)),
                  pl.BlockSpec((CHUNK, D_A), lambda b, c: (c, 0)),
                  pl.BlockSpec((CHUNK, D_A), lambda b, c: (c, 0)),
                  pl.BlockSpec((None, N_HEADS, CHUNK, CHUNK), lambda b, c: (l, 0, 0, 0)),
                  const((CHUNK, D_A)), const((1, D_A)), const((N_HEADS, CHUNK, CHUNK)),
                  const((CHUNK, D_BV)), const((CHUNK, D_A)), const((D_A, DV_B)), const((1, D_BV))],
        out_specs=[pl.BlockSpec((CHUNK, D_AB), lambda b, c: (b * n_c + c, 0)),
                   pl.BlockSpec((None, D_A, DV_B), lambda b, c: (b, 0, 0))],
        out_shape=[jax.ShapeDtypeStruct((bsz * t, D_AB), BF16),
                   jax.ShapeDtypeStruct((bsz, D_A, DV_B), F32)],
        compiler_params=_cparams(("parallel", "arbitrary")),
        name="mix_ab",
    )(z, cos_t, sin_t, p['sg_w_s'], bs_t, gain, dmask, qdec_t, kdec_t, sdec_t, gng)
    return yab, s_ret.reshape(bsz, N_HEADS, HEAD_DIM, DV_B)


def _softplus(x):
    return jnp.maximum(x, 0.0) + jnp.log(1.0 + jnp.exp(-jnp.abs(x)))


def _rwkv_vectors(zc, shifted, mu, w0, w2p, a0, a2p, g2, k_k, k_a):
    zs = zc + (shifted - zc) * mu
    r = zs[:, 0:D_C]
    k = zs[:, D_C:2 * D_C]
    v = zs[:, 2 * D_C:3 * D_C]
    xwa = zs[:, 3 * D_C:3 * D_C + 128]
    xg = zs[:, 3 * D_C + 128:C_IN]
    w_log = -_softplus(-(w0 + _dot(jnp.tanh(xwa), w2p))) - 0.5
    logw = -jnp.exp(w_log)
    a = jax.nn.sigmoid(a0 + _dot(xwa, a2p))
    g = _dot(jax.nn.sigmoid(xg), g2)
    kk = k * k_k
    sum_sel = _group_sel(D_C, D_C, HEAD_DIM, HEAD_DIM)
    kk = kk / jnp.maximum(jnp.sqrt(_dot_sel(kk * kk, sum_sel)), 1e-12)
    k2 = k * (1.0 + (a - 1.0) * k_a)
    return r, k2, v, logw, a, g, kk


def _rwkv_output(y, r, k2, v, g, r_k, ln_g, ln_b):
    mean_sel = _group_sel(D_C, D_C, HEAD_DIM, HEAD_DIM, 1.0 / HEAD_DIM)
    sum_sel = _group_sel(D_C, D_C, HEAD_DIM, HEAD_DIM)
    yc = y - _dot_sel(y, mean_sel)
    var = _dot_sel(yc * yc, mean_sel)
    yn = yc * lax.rsqrt(var + GN_EPS) * ln_g + ln_b
    bonus = _dot_sel(r * k2 * r_k, sum_sel) * v
    return (yn + bonus) * g


def _rwkv_kernel(zc_ref, mu_ref, w0_ref, w2p_ref, a0_ref, a2p_ref, g2_ref, kk_ref, ka_ref, rk_ref,
                 lng_ref, lnb_ref, yc_ref, s_ref, prev_ref):
    @pl.when(pl.program_id(1) == 0)
    def _():
        s_ref[...] = jnp.zeros_like(s_ref)
        prev_ref[...] = jnp.zeros_like(prev_ref)

    params = [ref[...] for ref in (mu_ref, w0_ref, w2p_ref, a0_ref, a2p_ref, g2_ref, kk_ref, ka_ref,
                                   rk_ref, lng_ref, lnb_ref)]
    n_seq = zc_ref.shape[0]
    outs, s_news, lasts = _rwkv_chunks([zc_ref[i] for i in range(n_seq)],
                                       [prev_ref[i] for i in range(n_seq)],
                                       [s_ref[i] for i in range(n_seq)], params)
    for i in range(n_seq):
        yc_ref[i] = outs[i]
        s_ref[i] = s_news[i]
        prev_ref[i] = lasts[i]


def _each(fn, *lists):
    return [fn(*args) for args in zip(*lists)]


def _rwkv_chunks(zcs, prevs, s_bds, params):
    mu, w0, w2p, a0, a2p, g2, k_k, k_a, r_k, ln_g, ln_b = params
    c_len = RW_CHUNK
    nw = N_HEADS * c_len
    row_id = lax.broadcasted_iota(jnp.int32, zcs[0].shape, 0)
    tri = jnp.where(lax.broadcasted_iota(jnp.int32, (c_len, c_len), 0)
                    >= lax.broadcasted_iota(jnp.int32, (c_len, c_len), 1), 1.0, 0.0).astype(BF16)
    t_id = lax.broadcasted_iota(jnp.int32, (c_len, nw), 0)
    s_id = lax.broadcasted_iota(jnp.int32, (c_len, nw), 1) % c_len
    strict = s_id < t_id
    incl = s_id <= t_id
    bd = _head_of((nw, nw), 0, c_len) == _head_of((nw, nw), 1, c_len)
    eye = lax.broadcasted_iota(jnp.int32, (nw, nw), 0) == lax.broadcasted_iota(jnp.int32, (nw, nw), 1)

    shifted = _each(lambda zc, prev: jnp.where(row_id == 0, prev, pltpu.roll(zc, 1, axis=0)), zcs, prevs)
    vecs = _each(lambda zc, sh: _rwkv_vectors(zc, sh, mu, w0, w2p, a0, a2p, g2, k_k, k_a), zcs, shifted)
    r, k2, v, logw, a, g, kk = [list(col) for col in zip(*vecs)]
    b = _each(lambda kk_i, a_i: kk_i * a_i, kk, a)

    def cumsum(lw):
        hi, mid, lo = _split3(lw)
        return (jnp.dot(tri, hi, preferred_element_type=F32) + jnp.dot(tri, mid, preferred_element_type=F32)
                + jnp.dot(tri, lo, preferred_element_type=F32))
    cum = _each(cumsum, logw)
    c_last = [c[c_len - 1:c_len, :] for c in cum]
    a_hat = _each(lambda kk_i, c, lw: -kk_i * jnp.exp(c - lw), kk, cum, logw)
    r_hat = _each(lambda r_i, c: r_i * jnp.exp(c), r, cum)
    bk_t = _each(lambda b_i, k_i, c: jnp.concatenate(
        [_stack_heads(b_i * jnp.exp(-c), HEAD_DIM), _stack_heads(k_i * jnp.exp(-c), HEAD_DIM)], axis=0),
        b, k2, cum)
    gram = _each(lambda ah, rh, bk: _dot3_t(jnp.concatenate([ah, rh], axis=0), bk), a_hat, r_hat, bk_t)
    x_w = [jnp.where(strict, gm[0:c_len, 0:nw], 0.0) for gm in gram]
    l_ak = [jnp.where(strict, gm[0:c_len, nw:2 * nw], 0.0) for gm in gram]
    m_rb = [jnp.where(incl, gm[c_len:2 * c_len, 0:nw], 0.0) for gm in gram]
    m_rk = [jnp.where(incl, gm[c_len:2 * c_len, nw:2 * nw], 0.0) for gm in gram]

    v_bd = [_stack_heads(v_i, HEAD_DIM) for v_i in v]
    w_rhs = _each(lambda ah, s, l, vb: _dot3_t(ah, s) + _dot3(l, vb), a_hat, s_bds, l_ak, v_bd)

    x_bd = [jnp.where(bd, jnp.concatenate([x] * N_HEADS, axis=0), 0.0) for x in x_w]
    power = [x.astype(BF16) for x in x_bd]
    t_inv = [jnp.where(eye, 1.0, x) for x in x_bd]
    for _ in range(5):
        power = [jnp.dot(pw, pw, preferred_element_type=F32).astype(BF16) for pw in power]
        t_inv = _each(lambda t, pw: t + jnp.dot(pw, t.astype(BF16), preferred_element_type=F32), t_inv, power)
    t_inv = [t.astype(BF16) for t in t_inv]
    w_bd = [_stack_heads(w, HEAD_DIM) for w in w_rhs]
    u_bd = _each(lambda t, w: jnp.dot(t, w.astype(BF16), preferred_element_type=F32), t_inv, w_bd)
    resid = _each(lambda w, u, x: w - u + _dot3(x, u), w_bd, u_bd, x_bd)
    u_bd = _each(lambda u, t, rs: u + jnp.dot(t, rs.astype(BF16), preferred_element_type=F32),
                 u_bd, t_inv, resid)
    u = [ub[0:c_len] + ub[c_len:2 * c_len] + ub[2 * c_len:3 * c_len] + ub[3 * c_len:4 * c_len] for ub in u_bd]

    y = _each(lambda rh, s, mb, ub, mk, vb: _dot3_t(rh, s) + _dot3(mb, ub) + _dot3(mk, vb),
              r_hat, s_bds, m_rb, u_bd, m_rk, v_bd)
    outs = _each(lambda y_i, r_i, k_i, v_i, g_i: _rwkv_output(y_i, r_i, k_i, v_i, g_i, r_k, ln_g, ln_b).astype(BF16),
                 y, r, k2, v, g)

    bd_s = _head_of(s_bds[0].shape, 0, HEAD_DIM) == _head_of(s_bds[0].shape, 1, HEAD_DIM)

    def new_state(s, u_i, v_i, b_i, k_i, c, cl):
        e_tail = jnp.exp(cl - c)
        uv_t = jnp.concatenate([u_i, v_i], axis=0).T
        s_upd = _dot3(uv_t, jnp.concatenate([b_i * e_tail, k_i * e_tail], axis=0))
        return s * jnp.exp(cl) + jnp.where(bd_s, s_upd, 0.0)
    s_news = _each(new_state, s_bds, u, v, b, k2, cum, c_last)
    lasts = [zc[c_len - 1:c_len, :] for zc in zcs]
    return outs, s_news, lasts


def _rwkv_params(l, p):
    zeros = jnp.zeros((HEAD_DIM, D_C), F32)
    w2p = jnp.concatenate([p['rw_w2'][l], zeros], axis=0).astype(BF16)
    a2p = jnp.concatenate([zeros, p['rw_a2'][l]], axis=0).astype(BF16)
    row = lambda a: a.reshape(1, -1)
    return [row(p['rw_mu'][l]), row(p['rw_w0'][l]), w2p, row(p['rw_a0'][l]), a2p,
            p['rw_g2'][l].astype(BF16), row(p['rw_k_k'][l]), row(p['rw_k_a'][l])]


def _rwkv_prompt(z, bsz, t, l, p):
    n_c = t // RW_CHUNK
    params = _rwkv_params(l, p) + [p['rw_r_k'][l].reshape(1, D_C), p['rw_ln_g'][l].reshape(1, D_C),
                                   p['rw_ln_b'][l].reshape(1, D_C)]
    const = lambda a: pl.BlockSpec(a.shape, lambda b, c: (0,) * a.ndim)
    nb = math.gcd(bsz, RW_SEQS_PER_STEP)
    yc, s_bd = pl.pallas_call(
        _rwkv_kernel,
        grid=(bsz // nb, n_c),
        in_specs=[pl.BlockSpec((nb, RW_CHUNK, C_IN), lambda b, c: (b, c, Z_AB // C_IN))]
                 + [const(a) for a in params],
        out_specs=[pl.BlockSpec((nb, RW_CHUNK, D_C), lambda b, c: (b, c, 0)),
                   pl.BlockSpec((nb, D_C, D_C), lambda b, c: (b, 0, 0))],
        out_shape=[jax.ShapeDtypeStruct((bsz, t, D_C), BF16),
                   jax.ShapeDtypeStruct((bsz, D_C, D_C), F32)],
        scratch_shapes=[pltpu.VMEM((nb, 1, C_IN), F32)],
        compiler_params=_cparams(("parallel", "arbitrary")),
        name="rwkv_chunk",
    )(z.reshape(bsz, t, D_IN), *params)
    yc = yc.reshape(bsz * t, D_C)
    s_heads = jnp.stack([s_bd[:, h * HEAD_DIM:(h + 1) * HEAD_DIM, h * HEAD_DIM:(h + 1) * HEAD_DIM]
                         for h in range(N_HEADS)], axis=1)
    return yc, s_heads


SAMPLE_SLOTS = 11


def _samp_prep_kernel(z_ref, prev_ref, cos_ref, sin_ref, ws0_ref, bs0_ref, gain_ref, mu_ref, w0_ref,
                      w2p_ref, a0_ref, a2p_ref, g2_ref, kk_ref, ka_ref, vec_ref, qk_ref):
    gu = _gelu(z_ref[:, 0:D_A])
    gv = _gelu(z_ref[:, D_A:2 * D_A])
    mean_sel = _group_sel(D_A, D_A, HEAD_DIM, HEAD_DIM, 1.0 / HEAD_DIM)
    vn = gv * lax.rsqrt(_dot_sel(gv * gv, mean_sel) + NORM_EPS) * gain_ref[...]
    vec_ref[0] = gu * (ws0_ref[...] * vn + bs0_ref[...])
    vec_ref[1] = vn
    q = _rotate_half(z_ref[:, 2 * D_A:3 * D_A], cos_ref[...], sin_ref[...])
    k = _rotate_half(z_ref[:, 3 * D_A:4 * D_A], cos_ref[...], sin_ref[...]) * (HEAD_DIM ** -0.5)
    vec_ref[2] = q
    vec_ref[3] = k
    qk_ref[...] = _dot_sel(q * k, _group_sel(D_A, D_BV, HEAD_DIM, DV_B))
    r, k2, v, logw, a, g, kk = _rwkv_vectors(z_ref[:, Z_AB:D_IN], prev_ref[...], mu_ref[...], w0_ref[...],
                                             w2p_ref[...], a0_ref[...], a2p_ref[...], g2_ref[...],
                                             kk_ref[...], ka_ref[...])
    vec_ref[4] = r
    vec_ref[5] = jnp.exp(logw)
    vec_ref[6] = k2
    vec_ref[7] = kk
    vec_ref[8] = kk * a
    vec_ref[9] = v
    vec_ref[10] = g


def _samp_state_kernel(sret_ref, swkv_ref, qc_ref, kc_ref, vr_ref, sdec_ref, w_ref, kk_ref, b_ref,
                       k2_ref, r_ref, vc_ref, sret_o_ref, cross_ref, swkv_o_ref, y_ref):
    s = sret_ref[...]
    cross_ref[...] = jnp.sum(s * qc_ref[...], axis=2, keepdims=True)
    sret_o_ref[...] = s * sdec_ref[...][None] + kc_ref[...] * vr_ref[...]
    sw = swkv_ref[...]
    sa = jnp.sum(sw * (-kk_ref[...]), axis=3, keepdims=True)
    sw = sw * w_ref[...] + sa * b_ref[...] + vc_ref[...] * k2_ref[...]
    swkv_o_ref[...] = sw
    y_ref[...] = jnp.sum(sw * r_ref[...], axis=3, keepdims=True)


def _samp_post_kernel(vec_ref, qk_ref, cross_ref, y_ref, zv_ref, zg_ref, qdec_ref, gng_ref, rk_ref,
                      lng_ref, lnb_ref, yab_ref, yc_ref):
    yab_ref[:, 0:D_A] = vec_ref[0].astype(BF16)
    ob = qk_ref[...] * zv_ref[...] + cross_ref[...] * qdec_ref[...]
    for h in range(N_HEADS):
        cols = slice(h * DV_B, (h + 1) * DV_B)
        yb = _rms(ob[:, cols]) * gng_ref[:, cols] * _silu(zg_ref[:, cols])
        yab_ref[:, D_A + h * DV_B:D_A + (h + 1) * DV_B] = yb.astype(BF16)
    out = _rwkv_output(y_ref[...], vec_ref[4], vec_ref[6], vec_ref[9], vec_ref[10], rk_ref[...],
                       lng_ref[...], lnb_ref[...])
    yc_ref[...] = out.astype(BF16)


def _mix_sample(z, s_ret, s_wkv, shift, l, p):
    bs = z.shape[0]
    bb = 8
    full = lambda a: pl.BlockSpec(a.shape, lambda *_: (0,) * a.ndim)
    cos_t, sin_t = _rope_tables(jnp.full((1,), PAST_LEN, F32))
    lg = _retention_log_decay()
    ws0 = jnp.repeat(p['sg_w_s'][l][:, 0, 0], HEAD_DIM).reshape(1, D_A)
    bs0 = jnp.repeat(p['sg_b_s'][l][:, 0], HEAD_DIM).reshape(1, D_A)
    gain = p['sg_v_gain'][l].reshape(1, D_A)
    prep_args = [z, shift, cos_t, sin_t, ws0, bs0, gain] + _rwkv_params(l, p)
    vec, qk = pl.pallas_call(
        _samp_prep_kernel,
        grid=(1,),
        in_specs=[full(a) for a in prep_args],
        out_specs=[pl.BlockSpec((SAMPLE_SLOTS, bs, D_A), lambda i: (0, 0, 0)),
                   pl.BlockSpec((bs, D_BV), lambda i: (0, 0))],
        out_shape=[jax.ShapeDtypeStruct((SAMPLE_SLOTS, bs, D_A), F32),
                   jax.ShapeDtypeStruct((bs, D_BV), F32)],
        compiler_params=_cparams(("arbitrary",)),
        name="sample_prep",
    )(*prep_args)

    zv = z[:, 4 * D_A:4 * D_A + D_BV]
    zg = z[:, 4 * D_A + D_BV:Z_AB]
    col = lambda a: a.reshape(bs, N_HEADS, HEAD_DIM, 1)
    rowv = lambda a: a.reshape(bs, N_HEADS, 1, HEAD_DIM)
    sdec_t = jnp.broadcast_to(jnp.exp(lg)[:, None, None], (N_HEADS, HEAD_DIM, DV_B))
    state_args = [s_ret, s_wkv, col(vec[2]), col(vec[3]), zv.reshape(bs, N_HEADS, 1, DV_B), sdec_t,
                  rowv(vec[5]), rowv(vec[7]), rowv(vec[8]), rowv(vec[6]), rowv(vec[4]), col(vec[9])]
    blk = lambda a: pl.BlockSpec((bb,) + a.shape[1:], lambda i: (i, 0, 0, 0))
    state_specs = [blk(a) for a in state_args]
    state_specs[5] = pl.BlockSpec(sdec_t.shape, lambda i: (0, 0, 0))
    out_shapes = [jax.ShapeDtypeStruct((bs, N_HEADS, HEAD_DIM, DV_B), F32),
                  jax.ShapeDtypeStruct((bs, N_HEADS, 1, DV_B), F32),
                  jax.ShapeDtypeStruct((bs, N_HEADS, HEAD_DIM, HEAD_DIM), F32),
                  jax.ShapeDtypeStruct((bs, N_HEADS, HEAD_DIM, 1), F32)]
    s_ret_new, cross, s_wkv_new, y = pl.pallas_call(
        _samp_state_kernel,
        grid=(bs // bb,),
        in_specs=state_specs,
        out_specs=[blk(a) for a in out_shapes],
        out_shape=out_shapes,
        compiler_params=_cparams(("parallel",)),
        name="sample_state",
    )(*state_args)

    qdec_t = jnp.repeat(jnp.exp(lg), DV_B).reshape(1, D_BV)
    post_args = [vec, qk, cross.reshape(bs, D_BV), y.reshape(bs, D_C), zv, zg, qdec_t,
                 p['ret_gn_g'][l].reshape(1, D_BV), p['rw_r_k'][l].reshape(1, D_C),
                 p['rw_ln_g'][l].reshape(1, D_C), p['rw_ln_b'][l].reshape(1, D_C)]
    yab, yc = pl.pallas_call(
        _samp_post_kernel,
        grid=(1,),
        in_specs=[full(a) for a in post_args],
        out_specs=[pl.BlockSpec((bs, D_AB), lambda i: (0, 0)), pl.BlockSpec((bs, D_C), lambda i: (0, 0))],
        out_shape=[jax.ShapeDtypeStruct((bs, D_AB), BF16), jax.ShapeDtypeStruct((bs, D_C), BF16)],
        compiler_params=_cparams(("arbitrary",)),
        name="sample_post",
    )(*post_args)
    return yab, yc, s_ret_new, s_wkv_new, vec[1]


def _row_tile(n_rows_per_seq):
    for t in (512, 256, 128):
        if n_rows_per_seq % t == 0:
            return t
    raise ValueError("sequence length must be a multiple of 128")


def kernel(x_prompt, x_sample, state_ret, state_wkv, state_shift, c_prompt, c_sample,
           w_ada, b_ada, w_ffn1_in, w_ffn1_out, w_in, w_out, w_ffn2_in, w_ffn2_out,
           sg_v_gain, sg_w_s, sg_b_s, ret_gn_g,
           rw_mu, rw_w0, rw_w2, rw_a0, rw_a2, rw_g2, rw_k_k, rw_k_a, rw_r_k, rw_ln_g, rw_ln_b,
           final_g):
    p = dict(sg_v_gain=sg_v_gain, sg_w_s=sg_w_s, sg_b_s=sg_b_s, ret_gn_g=ret_gn_g, rw_mu=rw_mu,
             rw_w0=rw_w0, rw_w2=rw_w2, rw_a0=rw_a0, rw_a2=rw_a2, rw_g2=rw_g2, rw_k_k=rw_k_k,
             rw_k_a=rw_k_a, rw_r_k=rw_r_k, rw_ln_g=rw_ln_g, rw_ln_b=rw_ln_b, final_g=final_g)
    wb = dict(ffn1_in=w_ffn1_in.astype(BF16), ffn1_out=w_ffn1_out.astype(BF16), w_in=w_in.astype(BF16),
              w_out=w_out.astype(BF16), ffn2_in=w_ffn2_in.astype(BF16), ffn2_out=w_ffn2_out.astype(BF16))
    bp, t, _ = x_prompt.shape
    bs = x_sample.shape[0]
    assert x_sample.shape[1] == 1 and t % CHUNK == 0

    mod = _ada(jnp.concatenate([c_prompt, c_sample], axis=0), w_ada, b_ada)
    mod = mod.reshape(DEPTH, bp + bs, N_MOD, D_MODEL)

    def prompt_mixer(z, l):
        yab, s_ret = _mix_ab_prompt(z, bp, t, l, p)
        yc, s_wkv = _rwkv_prompt(z, bp, t, l, p)
        last = z.reshape(bp, t, D_IN)[:, t - 1, Z_AB:]
        return yab, yc, (s_ret, s_wkv, last)

    tile_p = _row_tile(t)
    def run(x, make_rows, mixer):
        extras = []
        for l in range(DEPTH):
            rows = make_rows(l)
            x = _ffn(x, rows, l, 0, wb['ffn1_in'], wb['ffn1_out'])
            z = _proj_in(x, rows, l, wb['w_in'])
            yab, yc, extra = mixer(z, l)
            x = _proj_out(x, rows, l, yab, yc, wb['w_out'])
            x = _ffn(x, rows, l, 6, wb['ffn2_in'], wb['ffn2_out'],
                     final_g=final_g if l == DEPTH - 1 else None)
            extras.append(extra)
        return x, extras

    rows_p = lambda l: _Rows(mod[l, :bp].reshape(bp, N_MOD, 1, D_MODEL), bp * t, tile_p, t, False)
    y_p, ex_p = run(x_prompt.reshape(bp * t, D_MODEL), rows_p, prompt_mixer)

    def sample_mixer(z, l):
        yab, yc, s_ret, s_wkv, vn = _mix_sample(z, state_ret[l], state_wkv[l], state_shift[l], l, p)
        return yab, yc, (s_ret, s_wkv, z[:, Z_AB:], vn)

    rows_s = lambda l: _Rows(jnp.swapaxes(mod[l, bp:], 0, 1), bs, bs, 1, True)
    y_s, ex_s = run(x_sample.reshape(bs, D_MODEL), rows_s, sample_mixer)

    stack = lambda ex, i: jnp.stack([e[i] for e in ex])
    return (y_p.reshape(bp, t, D_MODEL), y_s.reshape(bs, 1, D_MODEL),
            stack(ex_p, 0), stack(ex_p, 1), stack(ex_p, 2),
            stack(ex_s, 0), stack(ex_s, 1), stack(ex_s, 2),
            stack(ex_s, 3).reshape(DEPTH, bs, 1, D_A))
```

```python
import functools
import math

import jax
import jax.numpy as jnp
from jax import lax
from jax.experimental import pallas as pl
from jax.experimental.pallas import tpu as pltpu

F32 = jnp.float32
BF16 = jnp.bfloat16

D_MODEL = 1024
DEPTH = 4
N_MOD = 9
D_FF = 2816
N_HEADS = 4
HEAD_DIM = 64
DV_B = 128
CHUNK = 128
RW_CHUNK = 64
AB_SEQS_PER_STEP = 4
RW_SEQS_PER_STEP = 8
D_A = N_HEADS * HEAD_DIM
D_BV = N_HEADS * DV_B
D_C = N_HEADS * HEAD_DIM
C_IN = 1024
D_IN = 3072
Z_AB = 2048
D_AB = D_A + D_BV
ROPE_BASE = 10000.0
NORM_EPS = 1e-6
GN_EPS = 64e-5
PAST_LEN = 16384

VMEM_LIMIT_BYTES = 56 * 1024 * 1024
MXU_WIDTH = 256
FFN_CHUNKS = ((0, 6 * MXU_WIDTH), (6 * MXU_WIDTH, 5 * MXU_WIDTH))


def _cparams(sem):
    return pltpu.CompilerParams(dimension_semantics=sem, vmem_limit_bytes=VMEM_LIMIT_BYTES)


def _dot(a, b):
    return jnp.dot(a.astype(BF16), b.astype(BF16), preferred_element_type=F32)


def _dot_t(a, b):
    return lax.dot_general(a.astype(BF16), b.astype(BF16), (((1,), (1,)), ((), ())),
                           preferred_element_type=F32)


def _split2(x):
    hi = x.astype(BF16)
    lo = (x - hi.astype(F32)).astype(BF16)
    return hi, lo


def _split3(x):
    hi = x.astype(BF16)
    r1 = x - hi.astype(F32)
    mid = r1.astype(BF16)
    lo = (r1 - mid.astype(F32)).astype(BF16)
    return hi, mid, lo


def _dot_sel(x, sel):
    hi, lo = _split2(x)
    return jnp.dot(hi, sel, preferred_element_type=F32) + jnp.dot(lo, sel, preferred_element_type=F32)


def _dot3(a, b):
    ah, al = _split2(a)
    bh, bl = _split2(b)
    return (jnp.dot(ah, bh, preferred_element_type=F32)
            + jnp.dot(ah, bl, preferred_element_type=F32)
            + jnp.dot(al, bh, preferred_element_type=F32))


def _dot3_t(a, b):
    ah, al = _split2(a)
    bh, bl = _split2(b)
    dn = (((1,), (1,)), ((), ()))
    return (lax.dot_general(ah, bh, dn, preferred_element_type=F32)
            + lax.dot_general(ah, bl, dn, preferred_element_type=F32)
            + lax.dot_general(al, bh, dn, preferred_element_type=F32))


def _rms(x, eps=NORM_EPS):
    return x * lax.rsqrt(jnp.mean(x * x, axis=-1, keepdims=True) + eps)


def _silu(x):
    return x * jax.nn.sigmoid(x)


def _gelu(x):
    return jax.nn.gelu(x)


def _head_of(shape, dim, width):
    return lax.broadcasted_iota(jnp.int32, shape, dim) // width


def _group_sel(n_in, n_out, w_in, w_out, scale=1.0):
    gi = _head_of((n_in, n_out), 0, w_in)
    go = _head_of((n_in, n_out), 1, w_out)
    return jnp.where(gi == go, scale, 0.0).astype(BF16)


def _stack_heads(x, width):
    lane_head = _head_of(x.shape, 1, width)
    return jnp.concatenate([jnp.where(lane_head == h, x, 0.0) for h in range(N_HEADS)], axis=0)


def _rotate_half(x, cos_t, sin_t):
    n = x.shape[1]
    half = HEAD_DIM // 2
    lane = lax.broadcasted_iota(jnp.int32, x.shape, 1)
    fwd = pltpu.roll(x, n - half, axis=1)
    bwd = pltpu.roll(x, half, axis=1)
    partner = jnp.where((lane % HEAD_DIM) < half, fwd, bwd)
    return x * cos_t + partner * sin_t


def _ada_kernel(c_ref, w_ref, b_ref, o_ref):
    c = c_ref[...]
    o_ref[...] = _dot(_silu(c), w_ref[...]) + b_ref[...]


def _ada(c_all, w_ada, b_ada):
    bt = c_all.shape[0]
    n_out = N_MOD * D_MODEL
    tn = 1152
    return pl.pallas_call(
        _ada_kernel,
        grid=(DEPTH, n_out // tn),
        in_specs=[pl.BlockSpec((bt, D_MODEL), lambda l, j: (0, 0)),
                  pl.BlockSpec((None, D_MODEL, tn), lambda l, j: (l, 0, j)),
                  pl.BlockSpec((None, 1, tn), lambda l, j: (l, 0, j))],
        out_specs=pl.BlockSpec((None, bt, tn), lambda l, j: (l, 0, j)),
        out_shape=jax.ShapeDtypeStruct((DEPTH, bt, n_out), F32),
        compiler_params=_cparams(("parallel", "parallel")),
        name="ada_mod",
    )(c_all, w_ada, b_ada.reshape(DEPTH, 1, n_out))


class _Rows:
    def __init__(self, mod, n_rows, tile, rows_per_seq, per_row):
        self.mod, self.n_rows, self.tile, self.per_row = mod, n_rows, tile, per_row
        self.n_tiles = n_rows // tile
        self.tiles_per_seq = None if per_row else rows_per_seq // tile

    def mod_spec(self, j):
        if self.per_row:
            return pl.BlockSpec((None, self.tile, D_MODEL), lambda i, *_: (j, i, 0))
        tps = self.tiles_per_seq
        return pl.BlockSpec((None, None, 1, D_MODEL), lambda i, *_: (i // tps, j, 0, 0))

    def row_spec(self, width):
        return pl.BlockSpec((self.tile, width), lambda i, *_: (i, 0))


def _ffn_kernel(*refs, final, mixer_out):
    refs = list(refs)
    o_ref = refs.pop()
    x_ref, sh_ref, sc_ref, g_ref, wi_ref, wo_ref = refs[:6]
    rest = refs[6:]
    x = x_ref[...]
    if mixer_out:
        gm_ref, yab_ref, yc_ref, wab_ref, wc_ref = rest[:5]
        rest = rest[5:]
        x = x + gm_ref[...] * (jnp.dot(yab_ref[...], wab_ref[...], preferred_element_type=F32)
                               + jnp.dot(yc_ref[...], wc_ref[...], preferred_element_type=F32))
    if final:
        fg_ref, = rest
    hb =(_rms(x) * (1.0 + sc_ref[...]) + sh_ref[...]).astype(BF16)
    y = None
    for c0, cw in FFN_CHUNKS:
        gate = jnp.dot(hb, wi_ref[:, c0:c0 + cw], preferred_element_type=F32)
        up = jnp.dot(hb, wi_ref[:, D_FF + c0:D_FF + c0 + cw], preferred_element_type=F32)
        part = jnp.dot((_silu(gate) * up).astype(BF16), wo_ref[c0:c0 + cw, :], preferred_element_type=F32)
        y = part if y is None else y + part
    out = x + 0.5 * g_ref[...] * y
    if final:
        out = _rms(out) * fg_ref[...]
    o_ref[...] = out


def _resident(shape, index_map):
    return pl.BlockSpec(shape, index_map, pipeline_mode=pl.Buffered(1))


def _ffn(x, rows, l, jm, w_in_b, w_out_b, final_g=None, mixer=None):
    final = final_g is not None
    in_specs = [rows.row_spec(D_MODEL), rows.mod_spec(jm), rows.mod_spec(jm + 1), rows.mod_spec(jm + 2),
                _resident((None, D_MODEL, 2 * D_FF), lambda i: (l, 0, 0)),
                _resident((None, D_FF, D_MODEL), lambda i: (l, 0, 0))]
    args = [x, rows.mod, rows.mod, rows.mod, w_in_b, w_out_b]
    if mixer is not None:
        yab, yc, w_mix = mixer
        in_specs += [rows.mod_spec(5), rows.row_spec(D_AB), rows.row_spec(D_C),
                     _resident((None, D_AB, D_MODEL), lambda i: (l, 0, 0)),
                     _resident((None, D_C, D_MODEL), lambda i: (l, D_AB // D_C, 0))]
        args += [rows.mod, yab, yc, w_mix, w_mix]
    if final:
        in_specs.append(pl.BlockSpec((1, D_MODEL), lambda i: (0, 0)))
        args.append(final_g.reshape(1, D_MODEL))
    return pl.pallas_call(
        functools.partial(_ffn_kernel, final=final, mixer_out=mixer is not None),
        grid=(rows.n_tiles,),
        in_specs=in_specs,
        out_specs=rows.row_spec(D_MODEL),
        out_shape=jax.ShapeDtypeStruct((rows.n_rows, D_MODEL), F32),
        compiler_params=_cparams(("parallel",)),
        name="ffn_final" if final else "ffn",
    )(*args)


def _proj_in_kernel(x_ref, sh_ref, sc_ref, w_ref, z_ref):
    h = (_rms(x_ref[...]) * (1.0 + sc_ref[...]) + sh_ref[...]).astype(BF16)
    z_ref[...] = jnp.dot(h, w_ref[...], preferred_element_type=F32)


def _proj_in(x, rows, l, w_in_b):
    return pl.pallas_call(
        _proj_in_kernel,
        grid=(rows.n_tiles,),
        in_specs=[rows.row_spec(D_MODEL), rows.mod_spec(3), rows.mod_spec(4),
                  pl.BlockSpec((None, D_MODEL, D_IN), lambda i: (l, 0, 0))],
        out_specs=rows.row_spec(D_IN),
        out_shape=jax.ShapeDtypeStruct((rows.n_rows, D_IN), F32),
        compiler_params=_cparams(("parallel",)),
        name="proj_in",
    )(x, rows.mod, rows.mod, w_in_b)


def _retention_log_decay():
    return jnp.log1p(-jnp.exp(jnp.linspace(math.log(1.0 / 32.0), math.log(1.0 / 512.0), N_HEADS, dtype=F32)))


def _rope_tables(pos):
    half = HEAD_DIM // 2
    inv = ROPE_BASE ** (-jnp.arange(half, dtype=F32) / half)
    ang = pos[:, None] * inv[None, :]
    cos, sin = jnp.cos(ang), jnp.sin(ang)
    cos_t = jnp.tile(jnp.concatenate([cos, cos], axis=-1), (1, N_HEADS))
    sin_t = jnp.tile(jnp.concatenate([-sin, sin], axis=-1), (1, N_HEADS))
    return cos_t, sin_t


def _retention_chunk_tables(c):
    lg = _retention_log_decay()
    idx = jnp.arange(c, dtype=F32)
    diff = idx[:, None] - idx[None, :]
    dmask = jnp.where(diff[None] >= 0, jnp.exp(jnp.maximum(diff, 0.0)[None] * lg[:, None, None]), 0.0)
    q_dec = jnp.exp((idx[:, None] + 1.0) * lg[None, :])
    k_dec = jnp.exp((c - 1.0 - idx)[:, None] * lg[None, :])
    s_dec = jnp.exp(c * lg)
    qdec_t = jnp.repeat(q_dec, DV_B, axis=1)
    kdec_t = jnp.repeat(k_dec, HEAD_DIM, axis=1)
    sdec_t = jnp.broadcast_to(jnp.repeat(s_dec, HEAD_DIM)[:, None], (N_HEADS * HEAD_DIM, DV_B))
    return dmask, qdec_t, kdec_t, sdec_t


def _mix_ab_kernel(z_ref, cos_ref, sin_ref, ws_ref, bs_ref, gain_ref, dmask_ref, qdec_ref, kdec_ref,
                   sdec_ref, gng_ref, yab_ref, s_ref):
    @pl.when(pl.program_id(1) == 0)
    def _():
        s_ref[...] = jnp.zeros_like(s_ref)

    seqs = range(z_ref.shape[0])
    mean_sel = _group_sel(D_A, D_A, HEAD_DIM, HEAD_DIM, 1.0 / HEAD_DIM)
    gv = [_gelu(z_ref[i, :, D_A:2 * D_A]) for i in seqs]
    vn = [g * lax.rsqrt(_dot_sel(g * g, mean_sel) + NORM_EPS) * gain_ref[...] for g in gv]
    lane_head = _head_of((CHUNK, D_A), 1, HEAD_DIM)
    row = lax.broadcasted_iota(jnp.int32, (CHUNK, CHUNK), 0)
    col = lax.broadcasted_iota(jnp.int32, (CHUNK, CHUNK), 1)
    mixed = [bs_ref[...] for _ in seqs]
    for h in range(N_HEADS):
        w_m = jnp.where(row >= col, ws_ref[h], 0.0).astype(BF16)
        mixed = [m + _dot(w_m, jnp.where(lane_head == h, v, 0.0)) for m, v in zip(mixed, vn)]
    for i in seqs:
        yab_ref[i, :, 0:D_A] = (_gelu(z_ref[i, :, 0:D_A]) * mixed[i]).astype(BF16)

    q = [_rotate_half(z_ref[i, :, 2 * D_A:3 * D_A], cos_ref[...], sin_ref[...]).astype(BF16) for i in seqs]
    k = [_rotate_half(z_ref[i, :, 3 * D_A:4 * D_A], cos_ref[...], sin_ref[...]) * (HEAD_DIM ** -0.5)
         for i in seqs]
    s_all = [s_ref[i] for i in seqs]
    kd_t = [(k_i * kdec_ref[...]).T for k_i in k]
    kb = [k_i.astype(BF16) for k_i in k]
    sb = [s.astype(BF16) for s in s_all]
    for h in range(N_HEADS):
        cols_v = slice(4 * D_A + h * DV_B, 4 * D_A + (h + 1) * DV_B)
        cols_g = slice(4 * D_A + D_BV + h * DV_B, 4 * D_A + D_BV + (h + 1) * DV_B)
        cols_h = slice(h * DV_B, (h + 1) * DV_B)
        rows_h = slice(h * HEAD_DIM, (h + 1) * HEAD_DIM)
        v_h = [z_ref[i, :, cols_v] for i in seqs]
        q_h = [jnp.where(lane_head == h, q_i, jnp.zeros_like(q_i)) for q_i in q]
        scores = [_dot_t(q_i, k_i) * dmask_ref[h] for q_i, k_i in zip(q_h, kb)]
        inner = [_dot(sc, v_i) for sc, v_i in zip(scores, v_h)]
        cross = [jnp.dot(q_i, s_i, preferred_element_type=F32) * qdec_ref[:, cols_h]
                 for q_i, s_i in zip(q_h, sb)]
        upd = [_dot(kd_i[rows_h, :], v_i) for kd_i, v_i in zip(kd_t, v_h)]
        for i in seqs:
            yb = _rms(inner[i] + cross[i]) * gng_ref[:, cols_h] * _silu(z_ref[i, :, cols_g])
            yab_ref[i, :, D_A + h * DV_B:D_A + (h + 1) * DV_B] = yb.astype(BF16)
            s_ref[i, rows_h, :] = s_all[i][rows_h, :] * sdec_ref[rows_h, :] + upd[i]


def _mix_ab_prompt(z, bsz, t, l, p):
    n_c = t // CHUNK
    pos = jnp.arange(t, dtype=F32)
    cos_t, sin_t = _rope_tables(pos)
    dmask, qdec_t, kdec_t, sdec_t = _retention_chunk_tables(CHUNK)
    bs_t = jnp.repeat(p['sg_b_s'][l].T, HEAD_DIM, axis=1)
    gain = p['sg_v_gain'][l].reshape(1, D_A)
    gng = p['ret_gn_g'][l].reshape(1, D_BV)
    const = lambda shape: pl.BlockSpec(shape, lambda b, c: (0,) * len(shape))
    nb = math.gcd(bsz, AB_SEQS_PER_STEP)
    yab, s_ret = pl.pallas_call(
        _mix_ab_kernel,
        grid=(bsz // nb, n_c),
        in_specs=[pl.BlockSpec((nb, CHUNK, Z_AB), lambda b, c: (b, c, 0)),
                  pl.BlockSpec((CHUNK, D_A), lambda b, c: (c, 0)),
                  pl.BlockSpec((CHUNK, D_A), lambda b, c: (c, 0)),
                  pl.BlockSpec((None, N_HEADS, CHUNK, CHUNK), lambda b, c: (l, 0, 0, 0)),
                  const((CHUNK, D_A)), const((1, D_A)), const((N_HEADS, CHUNK, CHUNK)),
                  const((CHUNK, D_BV)), const((CHUNK, D_A)), const((D_A, DV_B)), const((1, D_BV))],
        out_specs=[pl.BlockSpec((nb, CHUNK, D_AB), lambda b, c: (b, c, 0)),
                   pl.BlockSpec((nb, D_A, DV_B), lambda b, c: (b, 0, 0))],
        out_shape=[jax.ShapeDtypeStruct((bsz, t, D_AB), BF16),
                   jax.ShapeDtypeStruct((bsz, D_A, DV_B), F32)],
        compiler_params=_cparams(("parallel", "arbitrary")),
        name="mix_ab",
    )(z.reshape(bsz, t, D_IN), cos_t, sin_t, p['sg_w_s'], bs_t, gain, dmask, qdec_t, kdec_t, sdec_t, gng)
    return yab.reshape(bsz * t, D_AB), s_ret.reshape(bsz, N_HEADS, HEAD_DIM, DV_B)


def _softplus(x):
    return jnp.maximum(x, 0.0) + jnp.log(1.0 + jnp.exp(-jnp.abs(x)))


def _rwkv_vectors(zc, shifted, mu, w0, w2p, a0, a2p, g2, k_k, k_a):
    zs = zc + (shifted - zc) * mu
    r = zs[:, 0:D_C]
    k = zs[:, D_C:2 * D_C]
    v = zs[:, 2 * D_C:3 * D_C]
    xwa = zs[:, 3 * D_C:3 * D_C + 128]
    xg = zs[:, 3 * D_C + 128:C_IN]
    w_log = -_softplus(-(w0 + _dot(jnp.tanh(xwa), w2p))) - 0.5
    logw = -jnp.exp(w_log)
    a = jax.nn.sigmoid(a0 + _dot(xwa, a2p))
    g = _dot(jax.nn.sigmoid(xg), g2)
    kk = k * k_k
    sum_sel = _group_sel(D_C, D_C, HEAD_DIM, HEAD_DIM)
    kk = kk / jnp.maximum(jnp.sqrt(_dot_sel(kk * kk, sum_sel)), 1e-12)
    k2 = k * (1.0 + (a - 1.0) * k_a)
    return r, k2, v, logw, a, g, kk


def _rwkv_output(y, r, k2, v, g, r_k, ln_g, ln_b):
    mean_sel = _group_sel(D_C, D_C, HEAD_DIM, HEAD_DIM, 1.0 / HEAD_DIM)
    sum_sel = _group_sel(D_C, D_C, HEAD_DIM, HEAD_DIM)
    yc = y - _dot_sel(y, mean_sel)
    var = _dot_sel(yc * yc, mean_sel)
    yn = yc * lax.rsqrt(var + GN_EPS) * ln_g + ln_b
    bonus = _dot_sel(r * k2 * r_k, sum_sel) * v
    return (yn + bonus) * g


def _rwkv_kernel(zc_ref, mu_ref, w0_ref, w2p_ref, a0_ref, a2p_ref, g2_ref, kk_ref, ka_ref, rk_ref,
                 lng_ref, lnb_ref, yc_ref, s_ref, prev_ref):
    @pl.when(pl.program_id(1) == 0)
    def _():
        s_ref[...] = jnp.zeros_like(s_ref)
        prev_ref[...] = jnp.zeros_like(prev_ref)

    params = [ref[...] for ref in (mu_ref, w0_ref, w2p_ref, a0_ref, a2p_ref, g2_ref, kk_ref, ka_ref,
                                   rk_ref, lng_ref, lnb_ref)]
    n_seq = zc_ref.shape[0]
    outs, s_news, lasts = _rwkv_chunks([zc_ref[i] for i in range(n_seq)],
                                       [prev_ref[i] for i in range(n_seq)],
                                       [s_ref[i] for i in range(n_seq)], params)
    for i in range(n_seq):
        yc_ref[i] = outs[i]
        s_ref[i] = s_news[i]
        prev_ref[i] = lasts[i]


def _each(fn, *lists):
    return [fn(*args) for args in zip(*lists)]


def _rwkv_chunks(zcs, prevs, s_bds, params):
    mu, w0, w2p, a0, a2p, g2, k_k, k_a, r_k, ln_g, ln_b = params
    c_len = RW_CHUNK
    nw = N_HEADS * c_len
    row_id = lax.broadcasted_iota(jnp.int32, zcs[0].shape, 0)
    tri = jnp.where(lax.broadcasted_iota(jnp.int32, (c_len, c_len), 0)
                    >= lax.broadcasted_iota(jnp.int32, (c_len, c_len), 1), 1.0, 0.0).astype(BF16)
    t_id = lax.broadcasted_iota(jnp.int32, (c_len, nw), 0)
    s_id = lax.broadcasted_iota(jnp.int32, (c_len, nw), 1) % c_len
    strict = s_id < t_id
    incl = s_id <= t_id
    bd = _head_of((nw, nw), 0, c_len) == _head_of((nw, nw), 1, c_len)
    eye = lax.broadcasted_iota(jnp.int32, (nw, nw), 0) == lax.broadcasted_iota(jnp.int32, (nw, nw), 1)

    shifted = _each(lambda zc, prev: jnp.where(row_id == 0, prev, pltpu.roll(zc, 1, axis=0)), zcs, prevs)
    vecs = _each(lambda zc, sh: _rwkv_vectors(zc, sh, mu, w0, w2p, a0, a2p, g2, k_k, k_a), zcs, shifted)
    r, k2, v, logw, a, g, kk = [list(col) for col in zip(*vecs)]
    b = _each(lambda kk_i, a_i: kk_i * a_i, kk, a)

    def cumsum(lw):
        hi, mid, lo = _split3(lw)
        return (jnp.dot(tri, hi, preferred_element_type=F32) + jnp.dot(tri, mid, preferred_element_type=F32)
                + jnp.dot(tri, lo, preferred_element_type=F32))
    cum = _each(cumsum, logw)
    c_last = [c[c_len - 1:c_len, :] for c in cum]
    a_hat = _each(lambda kk_i, c, lw: -kk_i * jnp.exp(c - lw), kk, cum, logw)
    r_hat = _each(lambda r_i, c: r_i * jnp.exp(c), r, cum)
    bk_t = _each(lambda b_i, k_i, c: jnp.concatenate(
        [_stack_heads(b_i * jnp.exp(-c), HEAD_DIM), _stack_heads(k_i * jnp.exp(-c), HEAD_DIM)], axis=0),
        b, k2, cum)
    gram = _each(lambda ah, rh, bk: _dot3_t(jnp.concatenate([ah, rh], axis=0), bk), a_hat, r_hat, bk_t)
    x_w = [jnp.where(strict, gm[0:c_len, 0:nw], 0.0) for gm in gram]
    l_ak = [jnp.where(strict, gm[0:c_len, nw:2 * nw], 0.0) for gm in gram]
    m_rb = [jnp.where(incl, gm[c_len:2 * c_len, 0:nw], 0.0) for gm in gram]
    m_rk = [jnp.where(incl, gm[c_len:2 * c_len, nw:2 * nw], 0.0) for gm in gram]

    v_bd = [_stack_heads(v_i, HEAD_DIM) for v_i in v]
    w_rhs = _each(lambda ah, s, l, vb: _dot3_t(ah, s) + _dot3(l, vb), a_hat, s_bds, l_ak, v_bd)

    x_bd = [jnp.where(bd, jnp.concatenate([x] * N_HEADS, axis=0), 0.0) for x in x_w]
    power = [x.astype(BF16) for x in x_bd]
    t_inv = [jnp.where(eye, 1.0, x) for x in x_bd]
    for _ in range(5):
        power = [jnp.dot(pw, pw, preferred_element_type=F32).astype(BF16) for pw in power]
        t_inv = _each(lambda t, pw: t + jnp.dot(pw, t.astype(BF16), preferred_element_type=F32), t_inv, power)
    t_inv = [t.astype(BF16) for t in t_inv]
    w_bd = [_stack_heads(w, HEAD_DIM) for w in w_rhs]
    u_bd = _each(lambda t, w: jnp.dot(t, w.astype(BF16), preferred_element_type=F32), t_inv, w_bd)
    resid = _each(lambda w, u, x: w - u + _dot3(x, u), w_bd, u_bd, x_bd)
    u_bd = _each(lambda u, t, rs: u + jnp.dot(t, rs.astype(BF16), preferred_element_type=F32),
                 u_bd, t_inv, resid)
    u = [ub[0:c_len] + ub[c_len:2 * c_len] + ub[2 * c_len:3 * c_len] + ub[3 * c_len:4 * c_len] for ub in u_bd]

    y = _each(lambda rh, s, mb, ub, mk, vb: _dot3_t(rh, s) + _dot3(mb, ub) + _dot3(mk, vb),
              r_hat, s_bds, m_rb, u_bd, m_rk, v_bd)
    outs = _each(lambda y_i, r_i, k_i, v_i, g_i: _rwkv_output(y_i, r_i, k_i, v_i, g_i, r_k, ln_g, ln_b).astype(BF16),
                 y, r, k2, v, g)

    bd_s = _head_of(s_bds[0].shape, 0, HEAD_DIM) == _head_of(s_bds[0].shape, 1, HEAD_DIM)

    def new_state(s, u_i, v_i, b_i, k_i, c, cl):
        e_tail = jnp.exp(cl - c)
        uv_t = jnp.concatenate([u_i, v_i], axis=0).T
        s_upd = _dot3(uv_t, jnp.concatenate([b_i * e_tail, k_i * e_tail], axis=0))
        return s * jnp.exp(cl) + jnp.where(bd_s, s_upd, 0.0)
    s_news = _each(new_state, s_bds, u, v, b, k2, cum, c_last)
    lasts = [zc[c_len - 1:c_len, :] for zc in zcs]
    return outs, s_news, lasts


def _rwkv_params(l, p):
    zeros = jnp.zeros((HEAD_DIM, D_C), F32)
    w2p = jnp.concatenate([p['rw_w2'][l], zeros], axis=0).astype(BF16)
    a2p = jnp.concatenate([zeros, p['rw_a2'][l]], axis=0).astype(BF16)
    row = lambda a: a.reshape(1, -1)
    return [row(p['rw_mu'][l]), row(p['rw_w0'][l]), w2p, row(p['rw_a0'][l]), a2p,
            p['rw_g2'][l].astype(BF16), row(p['rw_k_k'][l]), row(p['rw_k_a'][l])]


def _rwkv_prompt(z, bsz, t, l, p):
    n_c = t // RW_CHUNK
    params = _rwkv_params(l, p) + [p['rw_r_k'][l].reshape(1, D_C), p['rw_ln_g'][l].reshape(1, D_C),
                                   p['rw_ln_b'][l].reshape(1, D_C)]
    const = lambda a: pl.BlockSpec(a.shape, lambda b, c: (0,) * a.ndim)
    nb = math.gcd(bsz, RW_SEQS_PER_STEP)
    yc, s_bd = pl.pallas_call(
        _rwkv_kernel,
        grid=(bsz // nb, n_c),
        in_specs=[pl.BlockSpec((nb, RW_CHUNK, C_IN), lambda b, c: (b, c, Z_AB // C_IN))]
                 + [const(a) for a in params],
        out_specs=[pl.BlockSpec((nb, RW_CHUNK, D_C), lambda b, c: (b, c, 0)),
                   pl.BlockSpec((nb, D_C, D_C), lambda b, c: (b, 0, 0))],
        out_shape=[jax.ShapeDtypeStruct((bsz, t, D_C), BF16),
                   jax.ShapeDtypeStruct((bsz, D_C, D_C), F32)],
        scratch_shapes=[pltpu.VMEM((nb, 1, C_IN), F32)],
        compiler_params=_cparams(("parallel", "arbitrary")),
        name="rwkv_chunk",
    )(z.reshape(bsz, t, D_IN), *params)
    yc = yc.reshape(bsz * t, D_C)
    s_heads = jnp.stack([s_bd[:, h * HEAD_DIM:(h + 1) * HEAD_DIM, h * HEAD_DIM:(h + 1) * HEAD_DIM]
                         for h in range(N_HEADS)], axis=1)
    return yc, s_heads


SAMPLE_SLOTS = 11


def _samp_prep_kernel(z_ref, prev_ref, cos_ref, sin_ref, ws0_ref, bs0_ref, gain_ref, mu_ref, w0_ref,
                      w2p_ref, a0_ref, a2p_ref, g2_ref, kk_ref, ka_ref, vec_ref, qk_ref):
    gu = _gelu(z_ref[:, 0:D_A])
    gv = _gelu(z_ref[:, D_A:2 * D_A])
    mean_sel = _group_sel(D_A, D_A, HEAD_DIM, HEAD_DIM, 1.0 / HEAD_DIM)
    vn = gv * lax.rsqrt(_dot_sel(gv * gv, mean_sel) + NORM_EPS) * gain_ref[...]
    vec_ref[0] = gu * (ws0_ref[...] * vn + bs0_ref[...])
    vec_ref[1] = vn
    q = _rotate_half(z_ref[:, 2 * D_A:3 * D_A], cos_ref[...], sin_ref[...])
    k = _rotate_half(z_ref[:, 3 * D_A:4 * D_A], cos_ref[...], sin_ref[...]) * (HEAD_DIM ** -0.5)
    vec_ref[2] = q
    vec_ref[3] = k
    qk_ref[...] = _dot_sel(q * k, _group_sel(D_A, D_BV, HEAD_DIM, DV_B))
    r, k2, v, logw, a, g, kk = _rwkv_vectors(z_ref[:, Z_AB:D_IN], prev_ref[...], mu_ref[...], w0_ref[...],
                                             w2p_ref[...], a0_ref[...], a2p_ref[...], g2_ref[...],
                                             kk_ref[...], ka_ref[...])
    vec_ref[4] = r
    vec_ref[5] = jnp.exp(logw)
    vec_ref[6] = k2
    vec_ref[7] = kk
    vec_ref[8] = kk * a
    vec_ref[9] = v
    vec_ref[10] = g


def _samp_state_kernel(sret_ref, swkv_ref, rows_ref, vr_ref, sdec_ref, sret_o_ref, cross_ref, swkv_o_ref,
                       y_ref):
    q, k, w, kk, b, k2, r, v = [rows_ref[:, :, i:i + 1, :] for i in range(8)]
    eye = (lax.broadcasted_iota(jnp.int32, (HEAD_DIM, HEAD_DIM), 0)
           == lax.broadcasted_iota(jnp.int32, (HEAD_DIM, HEAD_DIM), 1))

    def to_col(row):
        return jnp.sum(jnp.where(eye, row, 0.0), axis=3, keepdims=True)

    s = sret_ref[...]
    cross_ref[...] = jnp.sum(s * to_col(q), axis=2, keepdims=True)
    sret_o_ref[...] = s * sdec_ref[...][None] + to_col(k) * vr_ref[...]
    sw = swkv_ref[...]
    sa = jnp.sum(sw * (-kk), axis=3, keepdims=True)
    sw = sw * w + sa * b + to_col(v) * k2
    swkv_o_ref[...] = sw
    y_col = jnp.sum(sw * r, axis=3, keepdims=True)
    y_ref[...] = jnp.sum(jnp.where(eye, y_col, 0.0), axis=2, keepdims=True)


def _samp_post_kernel(vec_ref, qk_ref, cross_ref, y_ref, zv_ref, zg_ref, qdec_ref, gng_ref, rk_ref,
                      lng_ref, lnb_ref, yab_ref, yc_ref):
    yab_ref[:, 0:D_A] = vec_ref[0].astype(BF16)
    ob = qk_ref[...] * zv_ref[...] + cross_ref[...] * qdec_ref[...]
    for h in range(N_HEADS):
        cols = slice(h * DV_B, (h + 1) * DV_B)
        yb = _rms(ob[:, cols]) * gng_ref[:, cols] * _silu(zg_ref[:, cols])
        yab_ref[:, D_A + h * DV_B:D_A + (h + 1) * DV_B] = yb.astype(BF16)
    out = _rwkv_output(y_ref[...], vec_ref[4], vec_ref[6], vec_ref[9], vec_ref[10], rk_ref[...],
                       lng_ref[...], lnb_ref[...])
    yc_ref[...] = out.astype(BF16)


def _mix_sample(z, s_ret_all, s_wkv_all, shift, l, p):
    bs = z.shape[0]
    bb = 8
    full = lambda a: pl.BlockSpec(a.shape, lambda *_: (0,) * a.ndim)
    cos_t, sin_t = _rope_tables(jnp.full((1,), PAST_LEN, F32))
    lg = _retention_log_decay()
    ws0 = jnp.repeat(p['sg_w_s'][l][:, 0, 0], HEAD_DIM).reshape(1, D_A)
    bs0 = jnp.repeat(p['sg_b_s'][l][:, 0], HEAD_DIM).reshape(1, D_A)
    gain = p['sg_v_gain'][l].reshape(1, D_A)
    prep_args = [z, shift, cos_t, sin_t, ws0, bs0, gain] + _rwkv_params(l, p)
    vec, qk = pl.pallas_call(
        _samp_prep_kernel,
        grid=(1,),
        in_specs=[full(a) for a in prep_args],
        out_specs=[pl.BlockSpec((SAMPLE_SLOTS, bs, D_A), lambda i: (0, 0, 0)),
                   pl.BlockSpec((bs, D_BV), lambda i: (0, 0))],
        out_shape=[jax.ShapeDtypeStruct((SAMPLE_SLOTS, bs, D_A), F32),
                   jax.ShapeDtypeStruct((bs, D_BV), F32)],
        compiler_params=_cparams(("arbitrary",)),
        name="sample_prep",
    )(*prep_args)

    zv = z[:, 4 * D_A:4 * D_A + D_BV]
    zg = z[:, 4 * D_A + D_BV:Z_AB]
    sdec_t = jnp.broadcast_to(jnp.exp(lg)[:, None, None], (N_HEADS, HEAD_DIM, DV_B))
    rows = jnp.transpose(vec[jnp.array([2, 3, 5, 7, 8, 6, 4, 9])].reshape(8, bs, N_HEADS, HEAD_DIM),
                         (1, 2, 0, 3))
    blk = lambda shape: pl.BlockSpec((bb,) + shape[1:], lambda i: (i, 0, 0, 0))
    layer_blk = lambda shape: pl.BlockSpec((None, bb) + shape[2:], lambda i: (l, i, 0, 0, 0))
    out_shapes = [jax.ShapeDtypeStruct((bs, N_HEADS, HEAD_DIM, DV_B), F32),
                  jax.ShapeDtypeStruct((bs, N_HEADS, 1, DV_B), F32),
                  jax.ShapeDtypeStruct((bs, N_HEADS, HEAD_DIM, HEAD_DIM), F32),
                  jax.ShapeDtypeStruct((bs, N_HEADS, 1, HEAD_DIM), F32)]
    s_ret_new, cross, s_wkv_new, y = pl.pallas_call(
        _samp_state_kernel,
        grid=(bs // bb,),
        in_specs=[layer_blk(s_ret_all.shape), layer_blk(s_wkv_all.shape), blk(rows.shape),
                  blk((bs, N_HEADS, 1, DV_B)), pl.BlockSpec(sdec_t.shape, lambda i: (0, 0, 0))],
        out_specs=[blk(a.shape) for a in out_shapes],
        out_shape=out_shapes,
        compiler_params=_cparams(("parallel",)),
        name="sample_state",
    )(s_ret_all, s_wkv_all, rows, zv.reshape(bs, N_HEADS, 1, DV_B), sdec_t)

    qdec_t = jnp.repeat(jnp.exp(lg), DV_B).reshape(1, D_BV)
    post_args = [vec, qk, cross.reshape(bs, D_BV), y.reshape(bs, D_C), zv, zg, qdec_t,
                 p['ret_gn_g'][l].reshape(1, D_BV), p['rw_r_k'][l].reshape(1, D_C),
                 p['rw_ln_g'][l].reshape(1, D_C), p['rw_ln_b'][l].reshape(1, D_C)]
    yab, yc = pl.pallas_call(
        _samp_post_kernel,
        grid=(1,),
        in_specs=[full(a) for a in post_args],
        out_specs=[pl.BlockSpec((bs, D_AB), lambda i: (0, 0)), pl.BlockSpec((bs, D_C), lambda i: (0, 0))],
        out_shape=[jax.ShapeDtypeStruct((bs, D_AB), BF16), jax.ShapeDtypeStruct((bs, D_C), BF16)],
        compiler_params=_cparams(("arbitrary",)),
        name="sample_post",
    )(*post_args)
    return yab, yc, s_ret_new, s_wkv_new, vec[1]


def _row_tile(n_rows_per_seq):
    for t in (512, 256, 128):
        if n_rows_per_seq % t == 0:
            return t
    raise ValueError("sequence length must be a multiple of 128")


def kernel(x_prompt, x_sample, state_ret, state_wkv, state_shift, c_prompt, c_sample,
           w_ada, b_ada, w_ffn1_in, w_ffn1_out, w_in, w_out, w_ffn2_in, w_ffn2_out,
           sg_v_gain, sg_w_s, sg_b_s, ret_gn_g,
           rw_mu, rw_w0, rw_w2, rw_a0, rw_a2, rw_g2, rw_k_k, rw_k_a, rw_r_k, rw_ln_g, rw_ln_b,
           final_g):
    p = dict(sg_v_gain=sg_v_gain, sg_w_s=sg_w_s, sg_b_s=sg_b_s, ret_gn_g=ret_gn_g, rw_mu=rw_mu,
             rw_w0=rw_w0, rw_w2=rw_w2, rw_a0=rw_a0, rw_a2=rw_a2, rw_g2=rw_g2, rw_k_k=rw_k_k,
             rw_k_a=rw_k_a, rw_r_k=rw_r_k, rw_ln_g=rw_ln_g, rw_ln_b=rw_ln_b, final_g=final_g)
    wb = dict(ffn1_in=w_ffn1_in.astype(BF16), ffn1_out=w_ffn1_out.astype(BF16), w_in=w_in.astype(BF16),
              w_out=w_out.astype(BF16), ffn2_in=w_ffn2_in.astype(BF16), ffn2_out=w_ffn2_out.astype(BF16))
    bp, t, _ = x_prompt.shape
    bs = x_sample.shape[0]
    assert x_sample.shape[1] == 1 and t % CHUNK == 0

    mod = _ada(jnp.concatenate([c_prompt, c_sample], axis=0), w_ada, b_ada)
    mod = mod.reshape(DEPTH, bp + bs, N_MOD, D_MODEL)

    def prompt_mixer(z, l):
        yab, s_ret = _mix_ab_prompt(z, bp, t, l, p)
        yc, s_wkv = _rwkv_prompt(z, bp, t, l, p)
        last = z.reshape(bp, t, D_IN)[:, t - 1, Z_AB:]
        return yab, yc, (s_ret, s_wkv, last)

    tile_p = _row_tile(t)
    def run(x, make_rows, mixer):
        extras = []
        for l in range(DEPTH):
            rows = make_rows(l)
            x = _ffn(x, rows, l, 0, wb['ffn1_in'], wb['ffn1_out'])
            z = _proj_in(x, rows, l, wb['w_in'])
            yab, yc, extra = mixer(z, l)
            x = _ffn(x, rows, l, 6, wb['ffn2_in'], wb['ffn2_out'],
                     final_g=final_g if l == DEPTH - 1 else None, mixer=(yab, yc, wb['w_out']))
            extras.append(extra)
        return x, extras

    rows_p = lambda l: _Rows(mod[l, :bp].reshape(bp, N_MOD, 1, D_MODEL), bp * t, tile_p, t, False)
    y_p, ex_p = run(x_prompt.reshape(bp * t, D_MODEL), rows_p, prompt_mixer)

    def sample_mixer(z, l):
        yab, yc, s_ret, s_wkv, vn = _mix_sample(z, state_ret, state_wkv, state_shift[l], l, p)
        return yab, yc, (s_ret, s_wkv, z[:, Z_AB:], vn)

    rows_s = lambda l: _Rows(jnp.swapaxes(mod[l, bp:], 0, 1), bs, bs, 1, True)
    y_s, ex_s = run(x_sample.reshape(bs, D_MODEL), rows_s, sample_mixer)

    stack = lambda ex, i: jnp.stack([e[i] for e in ex])
    return (y_p.reshape(bp, t, D_MODEL), y_s.reshape(bs, 1, D_MODEL),
            stack(ex_p, 0), stack(ex_p, 1), stack(ex_p, 2),
            stack(ex_s, 0), stack(ex_s, 1), stack(ex_s, 2),
            stack(ex_s, 3).reshape(DEPTH, bs, 1, D_A))
```

```python
import functools
import math

import jax
import jax.numpy as jnp
from jax import lax
from jax.experimental import pallas as pl
from jax.experimental.pallas import tpu as pltpu

F32 = jnp.float32
BF16 = jnp.bfloat16

D_MODEL = 1024
DEPTH = 4
N_MOD = 9
D_FF = 2816
N_HEADS = 4
HEAD_DIM = 64
DV_B = 128
CHUNK = 128
RW_CHUNK = 64
AB_SEQS_PER_STEP = 4
RW_SEQS_PER_STEP = 8
D_A = N_HEADS * HEAD_DIM
D_BV = N_HEADS * DV_B
D_C = N_HEADS * HEAD_DIM
C_IN = 1024
D_IN = 3072
Z_AB = 2048
D_AB = D_A + D_BV
ROPE_BASE = 10000.0
NORM_EPS = 1e-6
GN_EPS = 64e-5
PAST_LEN = 16384

VMEM_LIMIT_BYTES = 56 * 1024 * 1024
SUBLANES = 8
MXU_WIDTH = 256
FFN_CHUNKS = ((0, 6 * MXU_WIDTH), (6 * MXU_WIDTH, 5 * MXU_WIDTH))


def _cparams(sem):
    return pltpu.CompilerParams(dimension_semantics=sem, vmem_limit_bytes=VMEM_LIMIT_BYTES)


def _dot(a, b):
    return jnp.dot(a.astype(BF16), b.astype(BF16), preferred_element_type=F32)


def _dot_t(a, b):
    return lax.dot_general(a.astype(BF16), b.astype(BF16), (((1,), (1,)), ((), ())),
                           preferred_element_type=F32)


def _split2(x):
    hi = x.astype(BF16)
    lo = (x - hi.astype(F32)).astype(BF16)
    return hi, lo


def _split3(x):
    hi = x.astype(BF16)
    r1 = x - hi.astype(F32)
    mid = r1.astype(BF16)
    lo = (r1 - mid.astype(F32)).astype(BF16)
    return hi, mid, lo


def _dot_sel(x, sel):
    m = x.shape[0]
    out = jnp.dot(jnp.concatenate(_split2(x), axis=0), sel, preferred_element_type=F32)
    return out[0:m] + out[m:2 * m]


def _dot3_general(a, b, dn):
    m = a.shape[0]
    ah, al = _split2(a)
    bh, bl = _split2(b)
    top = lax.dot_general(jnp.concatenate([ah, al], axis=0), bh, dn, preferred_element_type=F32)
    return top[0:m] + top[m:2 * m] + lax.dot_general(ah, bl, dn, preferred_element_type=F32)


def _dot3(a, b):
    return _dot3_general(a, b, (((1,), (0,)), ((), ())))


def _dot3_t(a, b):
    return _dot3_general(a, b, (((1,), (1,)), ((), ())))


def _rms(x, eps=NORM_EPS):
    return x * lax.rsqrt(jnp.mean(x * x, axis=-1, keepdims=True) + eps)


def _silu(x):
    return x * jax.nn.sigmoid(x)


def _gelu(x):
    return jax.nn.gelu(x)


def _head_of(shape, dim, width):
    return lax.broadcasted_iota(jnp.int32, shape, dim) // width


def _group_sel(n_in, n_out, w_in, w_out, scale=1.0):
    gi = _head_of((n_in, n_out), 0, w_in)
    go = _head_of((n_in, n_out), 1, w_out)
    return jnp.where(gi == go, scale, 0.0).astype(BF16)


def _stack_heads(x, width):
    lane_head = _head_of(x.shape, 1, width)
    return jnp.concatenate([jnp.where(lane_head == h, x, 0.0) for h in range(N_HEADS)], axis=0)


def _rotate_half(x, cos_t, sin_t):
    n = x.shape[1]
    half = HEAD_DIM // 2
    lane = lax.broadcasted_iota(jnp.int32, x.shape, 1)
    fwd = pltpu.roll(x, n - half, axis=1)
    bwd = pltpu.roll(x, half, axis=1)
    partner = jnp.where((lane % HEAD_DIM) < half, fwd, bwd)
    return x * cos_t + partner * sin_t


def _ada_kernel(c_ref, w_ref, b_ref, o_ref):
    c = c_ref[...]
    o_ref[...] = _dot(_silu(c), w_ref[...]) + b_ref[...]


def _ada(c_all, w_ada, b_ada):
    bt = c_all.shape[0]
    return pl.pallas_call(
        _ada_kernel,
        grid=(DEPTH, N_MOD),
        in_specs=[pl.BlockSpec((bt, D_MODEL), lambda l, j: (0, 0)),
                  pl.BlockSpec((None, D_MODEL, D_MODEL), lambda l, j: (l, 0, j)),
                  pl.BlockSpec((None, 1, D_MODEL), lambda l, j: (l, 0, j))],
        out_specs=pl.BlockSpec((None, None, bt, D_MODEL), lambda l, j: (l, j, 0, 0)),
        out_shape=jax.ShapeDtypeStruct((DEPTH, N_MOD, bt, D_MODEL), F32),
        compiler_params=_cparams(("parallel", "parallel")),
        name="ada_mod",
    )(c_all, w_ada, b_ada.reshape(DEPTH, 1, N_MOD * D_MODEL))


class _Rows:
    def __init__(self, mod, layer, n_rows, tile, tiles_per_seq=None, n_seq=None, seq_block=None):
        self.mod, self.layer, self.n_rows, self.tile = mod, layer, n_rows, tile
        self.n_tiles = n_rows // tile
        self.tiles_per_seq, self.n_seq, self.seq_block = tiles_per_seq, n_seq, seq_block

    def mod_spec(self, j):
        l = self.layer
        if self.tiles_per_seq is None:
            return pl.BlockSpec((None, None, self.tile, D_MODEL), lambda i: (l, j, i, 0))
        blk = self.seq_block
        return pl.BlockSpec((None, None, self.n_seq, D_MODEL), lambda i: (l, j, blk, 0))

    def row_spec(self, width):
        return pl.BlockSpec((self.tile, width), lambda i: (i, 0))


def _mod_row(ref, tiles_per_seq):
    if tiles_per_seq is None:
        return ref[...]
    return ref[pl.ds(pl.program_id(0) // tiles_per_seq, 1), :]


def _ffn_kernel(*refs, final, mixer_out, tiles_per_seq):
    refs = list(refs)
    o_ref = refs.pop()
    x_ref, sh_ref, sc_ref, g_ref, wi_ref, wo_ref = refs[:6]
    rest = refs[6:]
    mod = functools.partial(_mod_row, tiles_per_seq=tiles_per_seq)
    x = x_ref[...]
    if mixer_out:
        gm_ref, yab_ref, yc_ref, wab_ref, wc_ref = rest[:5]
        rest = rest[5:]
        x = x + mod(gm_ref) * (jnp.dot(yab_ref[...], wab_ref[...], preferred_element_type=F32)
                               + jnp.dot(yc_ref[...], wc_ref[...], preferred_element_type=F32))
    if final:
        fg_ref, = rest
    hb = (_rms(x) * (1.0 + mod(sc_ref)) + mod(sh_ref)).astype(BF16)
    y = None
    for c0, cw in FFN_CHUNKS:
        gate = jnp.dot(hb, wi_ref[:, c0:c0 + cw], preferred_element_type=F32)
        up = jnp.dot(hb, wi_ref[:, D_FF + c0:D_FF + c0 + cw], preferred_element_type=F32)
        part = jnp.dot((_silu(gate) * up).astype(BF16), wo_ref[c0:c0 + cw, :], preferred_element_type=F32)
        y = part if y is None else y + part
    out = x + 0.5 * mod(g_ref) * y
    if final:
        out = _rms(out) * fg_ref[...]
    o_ref[...] = out


def _resident(shape, index_map):
    return pl.BlockSpec(shape, index_map, pipeline_mode=pl.Buffered(1))


def _ffn(x, rows, l, jm, w_in_b, w_out_b, final_g=None, mixer=None):
    final = final_g is not None
    in_specs = [rows.row_spec(D_MODEL), rows.mod_spec(jm), rows.mod_spec(jm + 1), rows.mod_spec(jm + 2),
                _resident((None, D_MODEL, 2 * D_FF), lambda i: (l, 0, 0)),
                _resident((None, D_FF, D_MODEL), lambda i: (l, 0, 0))]
    args = [x, rows.mod, rows.mod, rows.mod, w_in_b, w_out_b]
    if mixer is not None:
        yab, yc, w_mix = mixer
        in_specs += [rows.mod_spec(5), rows.row_spec(D_AB), rows.row_spec(D_C),
                     _resident((None, D_AB, D_MODEL), lambda i: (l, 0, 0)),
                     _resident((None, D_C, D_MODEL), lambda i: (l, D_AB // D_C, 0))]
        args += [rows.mod, yab, yc, w_mix, w_mix]
    if final:
        in_specs.append(pl.BlockSpec((1, D_MODEL), lambda i: (0, 0)))
        args.append(final_g.reshape(1, D_MODEL))
    return pl.pallas_call(
        functools.partial(_ffn_kernel, final=final, mixer_out=mixer is not None,
                          tiles_per_seq=rows.tiles_per_seq),
        grid=(rows.n_tiles,),
        in_specs=in_specs,
        out_specs=rows.row_spec(D_MODEL),
        out_shape=jax.ShapeDtypeStruct((rows.n_rows, D_MODEL), F32),
        compiler_params=_cparams(("parallel",)),
        name="ffn_final" if final else "ffn",
    )(*args)


def _proj_in_kernel(x_ref, sh_ref, sc_ref, w_ref, z_ref, *, tiles_per_seq):
    sh, sc = _mod_row(sh_ref, tiles_per_seq), _mod_row(sc_ref, tiles_per_seq)
    h = (_rms(x_ref[...]) * (1.0 + sc) + sh).astype(BF16)
    z_ref[...] = jnp.dot(h, w_ref[...], preferred_element_type=F32)


def _proj_in(x, rows, l, w_in_b):
    return pl.pallas_call(
        functools.partial(_proj_in_kernel, tiles_per_seq=rows.tiles_per_seq),
        grid=(rows.n_tiles,),
        in_specs=[rows.row_spec(D_MODEL), rows.mod_spec(3), rows.mod_spec(4),
                  _resident((None, D_MODEL, D_IN), lambda i: (l, 0, 0))],
        out_specs=rows.row_spec(D_IN),
        out_shape=jax.ShapeDtypeStruct((rows.n_rows, D_IN), F32),
        compiler_params=_cparams(("parallel",)),
        name="proj_in",
    )(x, rows.mod, rows.mod, w_in_b)


def _retention_log_decay():
    return jnp.log1p(-jnp.exp(jnp.linspace(math.log(1.0 / 32.0), math.log(1.0 / 512.0), N_HEADS, dtype=F32)))


def _rope_tables(pos):
    half = HEAD_DIM // 2
    inv = ROPE_BASE ** (-jnp.arange(half, dtype=F32) / half)
    ang = pos[:, None] * inv[None, :]
    cos, sin = jnp.cos(ang), jnp.sin(ang)
    cos_t = jnp.tile(jnp.concatenate([cos, cos], axis=-1), (1, N_HEADS))
    sin_t = jnp.tile(jnp.concatenate([-sin, sin], axis=-1), (1, N_HEADS))
    return cos_t, sin_t


def _retention_chunk_tables(c):
    lg = _retention_log_decay()
    idx = jnp.arange(c, dtype=F32)
    diff = idx[:, None] - idx[None, :]
    dmask = jnp.where(diff[None] >= 0, jnp.exp(jnp.maximum(diff, 0.0)[None] * lg[:, None, None]), 0.0)
    q_dec = jnp.exp((idx[:, None] + 1.0) * lg[None, :])
    k_dec = jnp.exp((c - 1.0 - idx)[:, None] * lg[None, :])
    s_dec = jnp.exp(c * lg)
    qdec_t = jnp.repeat(q_dec, DV_B, axis=1)
    kdec_t = jnp.repeat(k_dec, HEAD_DIM, axis=1)
    sdec_t = jnp.broadcast_to(jnp.repeat(s_dec, HEAD_DIM)[:, None], (N_HEADS * HEAD_DIM, DV_B))
    return dmask, qdec_t, kdec_t, sdec_t


def _mix_ab_kernel(z_ref, cos_ref, sin_ref, ws_ref, bs_ref, gain_ref, dmask_ref, qdec_ref, kdec_ref,
                   sdec_ref, gng_ref, yab_ref, s_ref):
    @pl.when(pl.program_id(1) == 0)
    def _():
        s_ref[...] = jnp.zeros_like(s_ref)

    seqs = range(z_ref.shape[0])
    mean_sel = _group_sel(D_A, D_A, HEAD_DIM, HEAD_DIM, 1.0 / HEAD_DIM)
    gv = [_gelu(z_ref[i, :, D_A:2 * D_A]) for i in seqs]
    vn = [g * lax.rsqrt(_dot_sel(g * g, mean_sel) + NORM_EPS) * gain_ref[...] for g in gv]
    lane_head = _head_of((CHUNK, D_A), 1, HEAD_DIM)
    row = lax.broadcasted_iota(jnp.int32, (CHUNK, CHUNK), 0)
    col = lax.broadcasted_iota(jnp.int32, (CHUNK, CHUNK), 1)
    mixed = [bs_ref[...] for _ in seqs]
    for h in range(N_HEADS):
        w_m = jnp.where(row >= col, ws_ref[h], 0.0).astype(BF16)
        mixed = [m + _dot(w_m, jnp.where(lane_head == h, v, 0.0)) for m, v in zip(mixed, vn)]
    for i in seqs:
        yab_ref[i, :, 0:D_A] = (_gelu(z_ref[i, :, 0:D_A]) * mixed[i]).astype(BF16)

    q = [_rotate_half(z_ref[i, :, 2 * D_A:3 * D_A], cos_ref[...], sin_ref[...]).astype(BF16) for i in seqs]
    k = [_rotate_half(z_ref[i, :, 3 * D_A:4 * D_A], cos_ref[...], sin_ref[...]) * (HEAD_DIM ** -0.5)
         for i in seqs]
    s_all = [s_ref[i] for i in seqs]
    kd_t = [(k_i * kdec_ref[...]).T for k_i in k]
    kb = [k_i.astype(BF16) for k_i in k]
    sb = [s.astype(BF16) for s in s_all]
    for h in range(N_HEADS):
        cols_v = slice(4 * D_A + h * DV_B, 4 * D_A + (h + 1) * DV_B)
        cols_g = slice(4 * D_A + D_BV + h * DV_B, 4 * D_A + D_BV + (h + 1) * DV_B)
        cols_h = slice(h * DV_B, (h + 1) * DV_B)
        rows_h = slice(h * HEAD_DIM, (h + 1) * HEAD_DIM)
        v_h = [z_ref[i, :, cols_v] for i in seqs]
        q_h = [jnp.where(lane_head == h, q_i, jnp.zeros_like(q_i)) for q_i in q]
        scores = [_dot_t(q_i, k_i) * dmask_ref[h] for q_i, k_i in zip(q_h, kb)]
        inner = [_dot(sc, v_i) for sc, v_i in zip(scores, v_h)]
        cross = [jnp.dot(q_i, s_i, preferred_element_type=F32) * qdec_ref[:, cols_h]
                 for q_i, s_i in zip(q_h, sb)]
        upd = [_dot(kd_i[rows_h, :], v_i) for kd_i, v_i in zip(kd_t, v_h)]
        for i in seqs:
            yb = _rms(inner[i] + cross[i]) * gng_ref[:, cols_h] * _silu(z_ref[i, :, cols_g])
            yab_ref[i, :, D_A + h * DV_B:D_A + (h + 1) * DV_B] = yb.astype(BF16)
            s_ref[i, rows_h, :] = s_all[i][rows_h, :] * sdec_ref[rows_h, :] + upd[i]


def _mix_ab_prompt(z, bsz, t, l, p):
    n_c = t // CHUNK
    pos = jnp.arange(t, dtype=F32)
    cos_t, sin_t = _rope_tables(pos)
    dmask, qdec_t, kdec_t, sdec_t = _retention_chunk_tables(CHUNK)
    bs_t = jnp.repeat(p['sg_b_s'][l].T, HEAD_DIM, axis=1)
    gain = p['sg_v_gain'][l].reshape(1, D_A)
    gng = p['ret_gn_g'][l].reshape(1, D_BV)
    const = lambda shape: pl.BlockSpec(shape, lambda b, c: (0,) * len(shape))
    nb = math.gcd(bsz, AB_SEQS_PER_STEP)
    yab, s_ret = pl.pallas_call(
        _mix_ab_kernel,
        grid=(bsz // nb, n_c),
        in_specs=[pl.BlockSpec((nb, CHUNK, Z_AB), lambda b, c: (b, c, 0)),
                  pl.BlockSpec((CHUNK, D_A), lambda b, c: (c, 0)),
                  pl.BlockSpec((CHUNK, D_A), lambda b, c: (c, 0)),
                  pl.BlockSpec((None, N_HEADS, CHUNK, CHUNK), lambda b, c: (l, 0, 0, 0)),
                  const((CHUNK, D_A)), const((1, D_A)), const((N_HEADS, CHUNK, CHUNK)),
                  const((CHUNK, D_BV)), const((CHUNK, D_A)), const((D_A, DV_B)), const((1, D_BV))],
        out_specs=[pl.BlockSpec((nb, CHUNK, D_AB), lambda b, c: (b, c, 0)),
                   pl.BlockSpec((nb, D_A, DV_B), lambda b, c: (b, 0, 0))],
        out_shape=[jax.ShapeDtypeStruct((bsz, t, D_AB), BF16),
                   jax.ShapeDtypeStruct((bsz, D_A, DV_B), F32)],
        compiler_params=_cparams(("parallel", "arbitrary")),
        name="mix_ab",
    )(z.reshape(bsz, t, D_IN), cos_t, sin_t, p['sg_w_s'], bs_t, gain, dmask, qdec_t, kdec_t, sdec_t, gng)
    return yab.reshape(bsz * t, D_AB), s_ret.reshape(bsz, N_HEADS, HEAD_DIM, DV_B)


def _softplus(x):
    return jnp.maximum(x, 0.0) + jnp.log(1.0 + jnp.exp(-jnp.abs(x)))


def _rwkv_vectors(zc, shifted, mu, w0, w2p, a0, a2p, g2, k_k, k_a):
    zs = zc + (shifted - zc) * mu
    r = zs[:, 0:D_C]
    k = zs[:, D_C:2 * D_C]
    v = zs[:, 2 * D_C:3 * D_C]
    xwa = zs[:, 3 * D_C:3 * D_C + 128]
    xg = zs[:, 3 * D_C + 128:C_IN]
    w_log = -_softplus(-(w0 + _dot(jnp.tanh(xwa), w2p))) - 0.5
    logw = -jnp.exp(w_log)
    a = jax.nn.sigmoid(a0 + _dot(xwa, a2p))
    g = _dot(jax.nn.sigmoid(xg), g2)
    kk = k * k_k
    sum_sel = _group_sel(D_C, D_C, HEAD_DIM, HEAD_DIM)
    kk = kk / jnp.maximum(jnp.sqrt(_dot_sel(kk * kk, sum_sel)), 1e-12)
    k2 = k * (1.0 + (a - 1.0) * k_a)
    return r, k2, v, logw, a, g, kk


def _rwkv_output(y, r, k2, v, g, r_k, ln_g, ln_b):
    mean_sel = _group_sel(D_C, D_C, HEAD_DIM, HEAD_DIM, 1.0 / HEAD_DIM)
    sum_sel = _group_sel(D_C, D_C, HEAD_DIM, HEAD_DIM)
    yc = y - _dot_sel(y, mean_sel)
    var = _dot_sel(yc * yc, mean_sel)
    yn = yc * lax.rsqrt(var + GN_EPS) * ln_g + ln_b
    bonus = _dot_sel(r * k2 * r_k, sum_sel) * v
    return (yn + bonus) * g


def _rwkv_kernel(zc_ref, mu_ref, w0_ref, w2p_ref, a0_ref, a2p_ref, g2_ref, kk_ref, ka_ref, rk_ref,
                 lng_ref, lnb_ref, yc_ref, s_ref, prev_ref):
    @pl.when(pl.program_id(1) == 0)
    def _():
        s_ref[...] = jnp.zeros_like(s_ref)
        prev_ref[...] = jnp.zeros_like(prev_ref)

    params = [ref[...] for ref in (mu_ref, w0_ref, w2p_ref, a0_ref, a2p_ref, g2_ref, kk_ref, ka_ref,
                                   rk_ref, lng_ref, lnb_ref)]
    n_seq = zc_ref.shape[0]
    outs, s_news, lasts = _rwkv_chunks([zc_ref[i] for i in range(n_seq)],
                                       [prev_ref[i] for i in range(n_seq)],
                                       [s_ref[i] for i in range(n_seq)], params)
    for i in range(n_seq):
        yc_ref[i] = outs[i]
        s_ref[i] = s_news[i]
        prev_ref[i] = lasts[i]


def _each(fn, *lists):
    return [fn(*args) for args in zip(*lists)]


def _rwkv_chunks(zcs, prevs, s_bds, params):
    mu, w0, w2p, a0, a2p, g2, k_k, k_a, r_k, ln_g, ln_b = params
    c_len = RW_CHUNK
    nw = N_HEADS * c_len
    row_id = lax.broadcasted_iota(jnp.int32, zcs[0].shape, 0)
    tri = jnp.where(lax.broadcasted_iota(jnp.int32, (c_len, c_len), 0)
                    >= lax.broadcasted_iota(jnp.int32, (c_len, c_len), 1), 1.0, 0.0).astype(BF16)
    t_id = lax.broadcasted_iota(jnp.int32, (c_len, nw), 0)
    s_id = lax.broadcasted_iota(jnp.int32, (c_len, nw), 1) % c_len
    strict = s_id < t_id
    incl = s_id <= t_id
    bd = _head_of((nw, nw), 0, c_len) == _head_of((nw, nw), 1, c_len)

    n_seq = len(zcs)
    per_seq = lambda x: [x[i * c_len:(i + 1) * c_len] for i in range(n_seq)]
    shifted = _each(lambda zc, prev: jnp.where(row_id == 0, prev, pltpu.roll(zc, 1, axis=0)), zcs, prevs)
    vec_all = _rwkv_vectors(jnp.concatenate(zcs, axis=0), jnp.concatenate(shifted, axis=0),
                            mu, w0, w2p, a0, a2p, g2, k_k, k_a)
    r_all, k2_all, v_all, _, _, g_all, _ = vec_all
    r, k2, v, logw, a, g, kk = [per_seq(x) for x in vec_all]
    b = _each(lambda kk_i, a_i: kk_i * a_i, kk, a)

    def cumsum(lw):
        hi, mid, lo = _split3(lw)
        return (jnp.dot(tri, hi, preferred_element_type=F32) + jnp.dot(tri, mid, preferred_element_type=F32)
                + jnp.dot(tri, lo, preferred_element_type=F32))
    cum = _each(cumsum, logw)
    c_last = [c[c_len - 1:c_len, :] for c in cum]
    a_hat = _each(lambda kk_i, c, lw: -kk_i * jnp.exp(c - lw), kk, cum, logw)
    r_hat = _each(lambda r_i, c: r_i * jnp.exp(c), r, cum)
    bk_t = _each(lambda b_i, k_i, c: jnp.concatenate(
        [_stack_heads(b_i * jnp.exp(-c), HEAD_DIM), _stack_heads(k_i * jnp.exp(-c), HEAD_DIM)], axis=0),
        b, k2, cum)
    ar = _each(lambda ah, rh: jnp.concatenate([ah, rh], axis=0), a_hat, r_hat)
    gram = _each(_dot3_t, ar, bk_t)
    x_w = [jnp.where(strict, gm[0:c_len, 0:nw], 0.0) for gm in gram]
    l_ak = [jnp.where(strict, gm[0:c_len, nw:2 * nw], 0.0) for gm in gram]
    m_rb = [jnp.where(incl, gm[c_len:2 * c_len, 0:nw], 0.0) for gm in gram]
    m_rk = [jnp.where(incl, gm[c_len:2 * c_len, nw:2 * nw], 0.0) for gm in gram]

    v_bd = [_stack_heads(v_i, HEAD_DIM) for v_i in v]
    ar_s = _each(_dot3_t, ar, s_bds)
    lm_v = _each(lambda l, m, vb: _dot3(jnp.concatenate([l, m], axis=0), vb), l_ak, m_rk, v_bd)
    w_rhs = _each(lambda s_part, v_part: s_part[0:c_len] + v_part[0:c_len], ar_s, lm_v)

    def expand(m_w):
        return jnp.where(bd, jnp.concatenate([m_w] * N_HEADS, axis=0), jnp.zeros((), m_w.dtype))
    power = [x.astype(BF16) for x in x_w]
    t_inv = [jnp.where(s_id == t_id, 1.0, x) for x in x_w]
    for _ in range(5):
        power = [jnp.dot(pw, expand(pw), preferred_element_type=F32).astype(BF16) for pw in power]
        t_inv = _each(lambda t, pw: t + jnp.dot(pw, expand(t.astype(BF16)), preferred_element_type=F32),
                      t_inv, power)
    t_inv = [t.astype(BF16) for t in t_inv]
    u = _each(lambda t, w: jnp.dot(t, _stack_heads(w.astype(BF16), HEAD_DIM), preferred_element_type=F32),
              t_inv, w_rhs)
    resid = _each(lambda w, u_i, x: w - u_i + _dot3(x, _stack_heads(u_i, HEAD_DIM)), w_rhs, u, x_w)
    u = _each(lambda u_i, t, rs: u_i + jnp.dot(t, _stack_heads(rs.astype(BF16), HEAD_DIM),
                                               preferred_element_type=F32), u, t_inv, resid)
    u_bd = [_stack_heads(u_i, HEAD_DIM) for u_i in u]

    y = _each(lambda s_part, v_part, mb, ub: s_part[c_len:2 * c_len] + v_part[c_len:2 * c_len] + _dot3(mb, ub),
              ar_s, lm_v, m_rb, u_bd)
    outs = per_seq(_rwkv_output(jnp.concatenate(y, axis=0), r_all, k2_all, v_all, g_all,
                                r_k, ln_g, ln_b).astype(BF16))

    bd_s = _head_of(s_bds[0].shape, 0, HEAD_DIM) == _head_of(s_bds[0].shape, 1, HEAD_DIM)

    def new_state(s, u_i, v_i, b_i, k_i, c, cl):
        e_tail = jnp.exp(cl - c)
        uv_t = jnp.concatenate([u_i, v_i], axis=0).T
        s_upd = _dot3(uv_t, jnp.concatenate([b_i * e_tail, k_i * e_tail], axis=0))
        return s * jnp.exp(cl) + jnp.where(bd_s, s_upd, 0.0)
    s_news = _each(new_state, s_bds, u, v, b, k2, cum, c_last)
    lasts = [zc[c_len - 1:c_len, :] for zc in zcs]
    return outs, s_news, lasts


def _rwkv_params(l, p):
    zeros = jnp.zeros((HEAD_DIM, D_C), F32)
    w2p = jnp.concatenate([p['rw_w2'][l], zeros], axis=0).astype(BF16)
    a2p = jnp.concatenate([zeros, p['rw_a2'][l]], axis=0).astype(BF16)
    row = lambda a: a.reshape(1, -1)
    return [row(p['rw_mu'][l]), row(p['rw_w0'][l]), w2p, row(p['rw_a0'][l]), a2p,
            p['rw_g2'][l].astype(BF16), row(p['rw_k_k'][l]), row(p['rw_k_a'][l])]


def _rwkv_prompt(z, bsz, t, l, p):
    n_c = t // RW_CHUNK
    params = _rwkv_params(l, p) + [p['rw_r_k'][l].reshape(1, D_C), p['rw_ln_g'][l].reshape(1, D_C),
                                   p['rw_ln_b'][l].reshape(1, D_C)]
    const = lambda a: pl.BlockSpec(a.shape, lambda b, c: (0,) * a.ndim)
    nb = math.gcd(bsz, RW_SEQS_PER_STEP)
    yc, s_bd = pl.pallas_call(
        _rwkv_kernel,
        grid=(bsz // nb, n_c),
        in_specs=[pl.BlockSpec((nb, RW_CHUNK, C_IN), lambda b, c: (b, c, Z_AB // C_IN))]
                 + [const(a) for a in params],
        out_specs=[pl.BlockSpec((nb, RW_CHUNK, D_C), lambda b, c: (b, c, 0)),
                   pl.BlockSpec((nb, D_C, D_C), lambda b, c: (b, 0, 0))],
        out_shape=[jax.ShapeDtypeStruct((bsz, t, D_C), BF16),
                   jax.ShapeDtypeStruct((bsz, D_C, D_C), F32)],
        scratch_shapes=[pltpu.VMEM((nb, 1, C_IN), F32)],
        compiler_params=_cparams(("parallel", "arbitrary")),
        name="rwkv_chunk",
    )(z.reshape(bsz, t, D_IN), *params)
    yc = yc.reshape(bsz * t, D_C)
    s_heads = jnp.stack([s_bd[:, h * HEAD_DIM:(h + 1) * HEAD_DIM, h * HEAD_DIM:(h + 1) * HEAD_DIM]
                         for h in range(N_HEADS)], axis=1)
    return yc, s_heads


SAMPLE_SLOTS = 8
SAMPLE_WKV_VBLOCK = 16


def _samp_prep_kernel(z_ref, prev_ref, cos_ref, sin_ref, ws0_ref, bs0_ref, gain_ref, mu_ref, w0_ref,
                      w2p_ref, a0_ref, a2p_ref, g2_ref, kk_ref, ka_ref, vec_ref, qk_ref, vt_ref, v_t_ref):
    gu = _gelu(z_ref[:, 0:D_A])
    gv = _gelu(z_ref[:, D_A:2 * D_A])
    mean_sel = _group_sel(D_A, D_A, HEAD_DIM, HEAD_DIM, 1.0 / HEAD_DIM)
    vn = gv * lax.rsqrt(_dot_sel(gv * gv, mean_sel) + NORM_EPS) * gain_ref[...]
    vec_ref[0] = gu * (ws0_ref[...] * vn + bs0_ref[...])
    vec_ref[1] = vn
    q = _rotate_half(z_ref[:, 2 * D_A:3 * D_A], cos_ref[...], sin_ref[...])
    k = _rotate_half(z_ref[:, 3 * D_A:4 * D_A], cos_ref[...], sin_ref[...]) * (HEAD_DIM ** -0.5)
    vec_ref[2] = q
    vec_ref[3] = k
    qk_ref[...] = _dot_sel(q * k, _group_sel(D_A, D_BV, HEAD_DIM, DV_B))
    r, k2, v, logw, a, g, kk = _rwkv_vectors(z_ref[:, Z_AB:D_IN], prev_ref[...], mu_ref[...], w0_ref[...],
                                             w2p_ref[...], a0_ref[...], a2p_ref[...], g2_ref[...],
                                             kk_ref[...], ka_ref[...])
    vec_ref[4] = r
    vec_ref[5] = k2
    vec_ref[6] = v
    vec_ref[7] = g
    for i, x in enumerate((jnp.exp(logw), kk, kk * a, k2, r)):
        vt_ref[i] = x.T
    v_t_ref[...] = v.T


def _samp_wkv_kernel(s_ref, vt_ref, v_ref, so_ref, y_ref):
    w, kk, b, k2, r = [vt_ref[i] for i in range(5)]
    for j in range(s_ref.shape[0]):
        s = s_ref[j]
        sa = jnp.sum(s * (-kk), axis=0, keepdims=True)
        s = s * w + sa * b + v_ref[j:j + 1, :] * k2
        so_ref[j] = s
        y_ref[j:j + 1, :] = jnp.sum(s * r, axis=0, keepdims=True)


def _samp_state_kernel(sret_ref, rows_ref, vr_ref, sdec_ref, sret_o_ref, cross_ref):
    q, k = [rows_ref[:, :, i:i + 1, :] for i in range(2)]
    eye = (lax.broadcasted_iota(jnp.int32, (HEAD_DIM, HEAD_DIM), 0)
           == lax.broadcasted_iota(jnp.int32, (HEAD_DIM, HEAD_DIM), 1))

    def to_col(row):
        return jnp.sum(jnp.where(eye, row, 0.0), axis=3, keepdims=True)

    s = sret_ref[...]
    cross_ref[...] = jnp.sum(s * to_col(q), axis=2, keepdims=True)
    sret_o_ref[...] = s * sdec_ref[...][None] + to_col(k) * vr_ref[...]


def _samp_post_kernel(vec_ref, qk_ref, cross_ref, y_t_ref, zv_ref, zg_ref, qdec_ref, gng_ref, rk_ref,
                      lng_ref, lnb_ref, yab_ref, yc_ref):
    yab_ref[:, 0:D_A] = vec_ref[0].astype(BF16)
    ob = qk_ref[...] * zv_ref[...] + cross_ref[...] * qdec_ref[...]
    for h in range(N_HEADS):
        cols = slice(h * DV_B, (h + 1) * DV_B)
        yb = _rms(ob[:, cols]) * gng_ref[:, cols] * _silu(zg_ref[:, cols])
        yab_ref[:, D_A + h * DV_B:D_A + (h + 1) * DV_B] = yb.astype(BF16)
    out = _rwkv_output(y_t_ref[...].T, vec_ref[4], vec_ref[5], vec_ref[6], vec_ref[7], rk_ref[...],
                       lng_ref[...], lnb_ref[...])
    yc_ref[...] = out.astype(BF16)


def _mix_sample(z, s_ret_all, s_wkv_t_all, shift, l, p):
    bs = z.shape[0]
    bb = 8
    vb = SAMPLE_WKV_VBLOCK
    full = lambda a: pl.BlockSpec(a.shape, lambda *_: (0,) * a.ndim)
    cos_t, sin_t = _rope_tables(jnp.full((1,), PAST_LEN, F32))
    lg = _retention_log_decay()
    ws0 = jnp.repeat(p['sg_w_s'][l][:, 0, 0], HEAD_DIM).reshape(1, D_A)
    bs0 = jnp.repeat(p['sg_b_s'][l][:, 0], HEAD_DIM).reshape(1, D_A)
    gain = p['sg_v_gain'][l].reshape(1, D_A)
    prep_args = [z, shift, cos_t, sin_t, ws0, bs0, gain] + _rwkv_params(l, p)
    prep_shapes = [(SAMPLE_SLOTS, bs, D_A), (bs, D_BV), (5, D_C, bs), (D_C, bs)]
    vec, qk, vt, v_t = pl.pallas_call(
        _samp_prep_kernel,
        grid=(1,),
        in_specs=[full(a) for a in prep_args],
        out_specs=[pl.BlockSpec(s, lambda i, n=len(s): (0,) * n) for s in prep_shapes],
        out_shape=[jax.ShapeDtypeStruct(s, F32) for s in prep_shapes],
        compiler_params=_cparams(("arbitrary",)),
        name="sample_prep",
    )(*prep_args)

    s_wkv_new_t, y_t = pl.pallas_call(
        _samp_wkv_kernel,
        grid=(N_HEADS, HEAD_DIM // vb),
        in_specs=[pl.BlockSpec((None, None, vb, HEAD_DIM, bs), lambda h, i: (l, h, i, 0, 0)),
                  pl.BlockSpec((5, None, HEAD_DIM, bs), lambda h, i: (0, h, 0, 0)),
                  pl.BlockSpec((None, vb, bs), lambda h, i: (h, i, 0))],
        out_specs=[pl.BlockSpec((None, vb, HEAD_DIM, bs), lambda h, i: (h, i, 0, 0)),
                   pl.BlockSpec((None, vb, bs), lambda h, i: (h, i, 0))],
        out_shape=[jax.ShapeDtypeStruct((N_HEADS, HEAD_DIM, HEAD_DIM, bs), F32),
                   jax.ShapeDtypeStruct((N_HEADS, HEAD_DIM, bs), F32)],
        compiler_params=_cparams(("parallel", "parallel")),
        name="sample_wkv",
    )(s_wkv_t_all, vt.reshape(5, N_HEADS, HEAD_DIM, bs), v_t.reshape(N_HEADS, HEAD_DIM, bs))

    zv = z[:, 4 * D_A:4 * D_A + D_BV]
    zg = z[:, 4 * D_A + D_BV:Z_AB]
    sdec_t = jnp.broadcast_to(jnp.exp(lg)[:, None, None], (N_HEADS, HEAD_DIM, DV_B))
    rows = jnp.transpose(vec[2:4].reshape(2, bs, N_HEADS, HEAD_DIM), (1, 2, 0, 3))
    blk = lambda shape: pl.BlockSpec((bb,) + shape[1:], lambda i: (i, 0, 0, 0))
    out_shapes = [jax.ShapeDtypeStruct((bs, N_HEADS, HEAD_DIM, DV_B), F32),
                  jax.ShapeDtypeStruct((bs, N_HEADS, 1, DV_B), F32)]
    s_ret_new, cross = pl.pallas_call(
        _samp_state_kernel,
        grid=(bs // bb,),
        in_specs=[pl.BlockSpec((None, bb) + s_ret_all.shape[2:], lambda i: (l, i, 0, 0, 0)),
                  blk(rows.shape), blk((bs, N_HEADS, 1, DV_B)),
                  pl.BlockSpec(sdec_t.shape, lambda i: (0, 0, 0))],
        out_specs=[blk(a.shape) for a in out_shapes],
        out_shape=out_shapes,
        compiler_params=_cparams(("parallel",)),
        name="sample_state",
    )(s_ret_all, rows, zv.reshape(bs, N_HEADS, 1, DV_B), sdec_t)

    qdec_t = jnp.repeat(jnp.exp(lg), DV_B).reshape(1, D_BV)
    post_args = [vec, qk, cross.reshape(bs, D_BV), y_t.reshape(D_C, bs), zv, zg, qdec_t,
                 p['ret_gn_g'][l].reshape(1, D_BV), p['rw_r_k'][l].reshape(1, D_C),
                 p['rw_ln_g'][l].reshape(1, D_C), p['rw_ln_b'][l].reshape(1, D_C)]
    yab, yc = pl.pallas_call(
        _samp_post_kernel,
        grid=(1,),
        in_specs=[full(a) for a in post_args],
        out_specs=[pl.BlockSpec((bs, D_AB), lambda i: (0, 0)), pl.BlockSpec((bs, D_C), lambda i: (0, 0))],
        out_shape=[jax.ShapeDtypeStruct((bs, D_AB), BF16), jax.ShapeDtypeStruct((bs, D_C), BF16)],
        compiler_params=_cparams(("arbitrary",)),
        name="sample_post",
    )(*post_args)
    return yab, yc, s_ret_new, s_wkv_new_t, vec[1]


def _row_tile(n_rows_per_seq):
    for t in (512, 256, 128):
        if n_rows_per_seq % t == 0:
            return t
    raise ValueError("sequence length must be a multiple of 128")


def kernel(x_prompt, x_sample, state_ret, state_wkv, state_shift, c_prompt, c_sample,
           w_ada, b_ada, w_ffn1_in, w_ffn1_out, w_in, w_out, w_ffn2_in, w_ffn2_out,
           sg_v_gain, sg_w_s, sg_b_s, ret_gn_g,
           rw_mu, rw_w0, rw_w2, rw_a0, rw_a2, rw_g2, rw_k_k, rw_k_a, rw_r_k, rw_ln_g, rw_ln_b,
           final_g):
    p = dict(sg_v_gain=sg_v_gain, sg_w_s=sg_w_s, sg_b_s=sg_b_s, ret_gn_g=ret_gn_g, rw_mu=rw_mu,
             rw_w0=rw_w0, rw_w2=rw_w2, rw_a0=rw_a0, rw_a2=rw_a2, rw_g2=rw_g2, rw_k_k=rw_k_k,
             rw_k_a=rw_k_a, rw_r_k=rw_r_k, rw_ln_g=rw_ln_g, rw_ln_b=rw_ln_b, final_g=final_g)
    wb = dict(ffn1_in=w_ffn1_in.astype(BF16), ffn1_out=w_ffn1_out.astype(BF16), w_in=w_in.astype(BF16),
              w_out=w_out.astype(BF16), ffn2_in=w_ffn2_in.astype(BF16), ffn2_out=w_ffn2_out.astype(BF16))
    bp, t, _ = x_prompt.shape
    bs = x_sample.shape[0]
    assert x_sample.shape[1] == 1 and t % CHUNK == 0

    bp_pad = -(-bp // SUBLANES) * SUBLANES
    assert bs % bp_pad == 0
    c_all = jnp.concatenate([c_sample, c_prompt, jnp.zeros((bp_pad - bp, D_MODEL), F32)], axis=0)
    mod = _ada(c_all, w_ada, b_ada)

    def prompt_mixer(z, l):
        yab, s_ret = _mix_ab_prompt(z, bp, t, l, p)
        yc, s_wkv = _rwkv_prompt(z, bp, t, l, p)
        last = z.reshape(bp, t, D_IN)[:, t - 1, Z_AB:]
        return yab, yc, (s_ret, s_wkv, last)

    tile_p = _row_tile(t)
    def run(x, make_rows, mixer):
        extras = []
        for l in range(DEPTH):
            rows = make_rows(l)
            x = _ffn(x, rows, l, 0, wb['ffn1_in'], wb['ffn1_out'])
            z = _proj_in(x, rows, l, wb['w_in'])
            yab, yc, extra = mixer(z, l)
            x = _ffn(x, rows, l, 6, wb['ffn2_in'], wb['ffn2_out'],
                     final_g=final_g if l == DEPTH - 1 else None, mixer=(yab, yc, wb['w_out']))
            extras.append(extra)
        return x, extras

    rows_p = lambda l: _Rows(mod, l, bp * t, tile_p, tiles_per_seq=t // tile_p, n_seq=bp_pad,
                             seq_block=bs // bp_pad)
    y_p, ex_p = run(x_prompt.reshape(bp * t, D_MODEL), rows_p, prompt_mixer)

    state_wkv_t = jnp.transpose(state_wkv, (0, 2, 3, 4, 1))

    def sample_mixer(z, l):
        yab, yc, s_ret, s_wkv_t, vn = _mix_sample(z, state_ret, state_wkv_t, state_shift[l], l, p)
        return yab, yc, (s_ret, s_wkv_t, z[:, Z_AB:], vn)

    rows_s = lambda l: _Rows(mod, l, bs, bs)
    y_s, ex_s = run(x_sample.reshape(bs, D_MODEL), rows_s, sample_mixer)

    stack = lambda ex, i: jnp.stack([e[i] for e in ex])
    return (y_p.reshape(bp, t, D_MODEL), y_s.reshape(bs, 1, D_MODEL),
            stack(ex_p, 0), stack(ex_p, 1), stack(ex_p, 2),
            stack(ex_s, 0), jnp.transpose(stack(ex_s, 1), (0, 4, 1, 2, 3)), stack(ex_s, 2),
            stack(ex_s, 3).reshape(DEPTH, bs, 1, D_A))
```

```python
import functools
import math

import jax
import jax.numpy as jnp
from jax import lax
from jax.experimental import pallas as pl
from jax.experimental.pallas import tpu as pltpu

F32 = jnp.float32
BF16 = jnp.bfloat16

D_MODEL = 1024
DEPTH = 4
N_MOD = 9
D_FF = 2816
N_HEADS = 4
HEAD_DIM = 64
DV_B = 128
CHUNK = 128
RW_CHUNK = 64
AB_SEQS_PER_STEP = 8
RW_SEQS_PER_STEP = 8
D_A = N_HEADS * HEAD_DIM
D_BV = N_HEADS * DV_B
D_C = N_HEADS * HEAD_DIM
C_IN = 1024
D_IN = 3072
Z_AB = 2048
D_AB = D_A + D_BV
ROPE_BASE = 10000.0
NORM_EPS = 1e-6
GN_EPS = 64e-5
PAST_LEN = 16384

VMEM_LIMIT_BYTES = 56 * 1024 * 1024
SUBLANES = 8
MXU_WIDTH = 256
FFN_CHUNKS = ((0, 6 * MXU_WIDTH), (6 * MXU_WIDTH, 5 * MXU_WIDTH))


def _cparams(sem):
    return pltpu.CompilerParams(dimension_semantics=sem, vmem_limit_bytes=VMEM_LIMIT_BYTES)


def _dot(a, b):
    return jnp.dot(a.astype(BF16), b.astype(BF16), preferred_element_type=F32)


def _dot_t(a, b):
    return lax.dot_general(a.astype(BF16), b.astype(BF16), (((1,), (1,)), ((), ())),
                           preferred_element_type=F32)


def _split2(x):
    hi = x.astype(BF16)
    lo = (x - hi.astype(F32)).astype(BF16)
    return hi, lo


def _split3(x):
    hi = x.astype(BF16)
    r1 = x - hi.astype(F32)
    mid = r1.astype(BF16)
    lo = (r1 - mid.astype(F32)).astype(BF16)
    return hi, mid, lo


def _dot_sel(x, sel):
    m = x.shape[0]
    out = jnp.dot(jnp.concatenate(_split2(x), axis=0), sel, preferred_element_type=F32)
    return out[0:m] + out[m:2 * m]


def _dot3_general(a, b, dn):
    ah, al = a if isinstance(a, tuple) else _split2(a)
    bh, bl = b if isinstance(b, tuple) else _split2(b)
    m = ah.shape[0]
    top = lax.dot_general(jnp.concatenate([ah, al], axis=0), bh, dn, preferred_element_type=F32)
    return top[0:m] + top[m:2 * m] + lax.dot_general(ah, bl, dn, preferred_element_type=F32)


def _dot3(a, b):
    return _dot3_general(a, b, (((1,), (0,)), ((), ())))


def _dot3_t(a, b):
    return _dot3_general(a, b, (((1,), (1,)), ((), ())))


def _rms(x, eps=NORM_EPS):
    return x * lax.rsqrt(jnp.mean(x * x, axis=-1, keepdims=True) + eps)


def _silu(x):
    return x * jax.nn.sigmoid(x)


def _gelu(x):
    c = math.sqrt(2.0 / math.pi)
    return (0.5 * x) * (1.0 + jnp.tanh(x * (c + (c * 0.044715) * (x * x))))


def _head_of(shape, dim, width):
    return lax.broadcasted_iota(jnp.int32, shape, dim) // width


def _group_sel(n_in, n_out, w_in, w_out, scale=1.0):
    gi = _head_of((n_in, n_out), 0, w_in)
    go = _head_of((n_in, n_out), 1, w_out)
    return jnp.where(gi == go, scale, 0.0).astype(BF16)


def _stack_heads(x, width):
    lane_head = _head_of(x.shape, 1, width)
    return jnp.concatenate([jnp.where(lane_head == h, x, 0.0) for h in range(N_HEADS)], axis=0)


def _rotate_half(x, cos_t, sin_t):
    n = x.shape[1]
    half = HEAD_DIM // 2
    lane = lax.broadcasted_iota(jnp.int32, x.shape, 1)
    fwd = pltpu.roll(x, n - half, axis=1)
    bwd = pltpu.roll(x, half, axis=1)
    partner = jnp.where((lane % HEAD_DIM) < half, fwd, bwd)
    return x * cos_t + partner * sin_t


def _ada_kernel(c_ref, w_ref, b_ref, o_ref):
    c = c_ref[...]
    o_ref[...] = _dot(_silu(c), w_ref[...]) + b_ref[...]


def _ada(c_all, w_ada, b_ada):
    bt = c_all.shape[0]
    return pl.pallas_call(
        _ada_kernel,
        grid=(DEPTH, N_MOD),
        in_specs=[pl.BlockSpec((bt, D_MODEL), lambda l, j: (0, 0)),
                  pl.BlockSpec((None, D_MODEL, D_MODEL), lambda l, j: (l, 0, j)),
                  pl.BlockSpec((None, 1, D_MODEL), lambda l, j: (l, 0, j))],
        out_specs=pl.BlockSpec((None, None, bt, D_MODEL), lambda l, j: (l, j, 0, 0)),
        out_shape=jax.ShapeDtypeStruct((DEPTH, N_MOD, bt, D_MODEL), F32),
        compiler_params=_cparams(("parallel", "parallel")),
        name="ada_mod",
    )(c_all, w_ada, b_ada.reshape(DEPTH, 1, N_MOD * D_MODEL))


class _Rows:
    def __init__(self, mod, layer, n_rows, tile, tiles_per_seq=None, n_seq=None, seq_block=None):
        self.mod, self.layer, self.n_rows, self.tile = mod, layer, n_rows, tile
        self.n_tiles = n_rows // tile
        self.tiles_per_seq, self.n_seq, self.seq_block = tiles_per_seq, n_seq, seq_block

    def mod_spec(self, j):
        l = self.layer
        if self.tiles_per_seq is None:
            return pl.BlockSpec((None, None, self.tile, D_MODEL), lambda i: (l, j, i, 0))
        blk = self.seq_block
        return pl.BlockSpec((None, None, self.n_seq, D_MODEL), lambda i: (l, j, blk, 0))

    def row_spec(self, width):
        return pl.BlockSpec((self.tile, width), lambda i: (i, 0))


def _mod_row(ref, tiles_per_seq):
    if tiles_per_seq is None:
        return ref[...]
    return ref[pl.ds(pl.program_id(0) // tiles_per_seq, 1), :]


def _ffn_kernel(*refs, final, mixer_out, tiles_per_seq):
    refs = list(refs)
    o_ref = refs.pop()
    x_ref, sh_ref, sc_ref, g_ref, wi_ref, wo_ref = refs[:6]
    rest = refs[6:]
    mod = functools.partial(_mod_row, tiles_per_seq=tiles_per_seq)
    x = x_ref[...]
    if mixer_out:
        gm_ref, yab_ref, yc_ref, wab_ref, wc_ref = rest[:5]
        rest = rest[5:]
        x = x + mod(gm_ref) * (jnp.dot(yab_ref[...], wab_ref[...], preferred_element_type=F32)
                               + jnp.dot(yc_ref[...], wc_ref[...], preferred_element_type=F32))
    if final:
        fg_ref, = rest
    hb = (_rms(x) * (1.0 + mod(sc_ref)) + mod(sh_ref)).astype(BF16)
    y = None
    for c0, cw in FFN_CHUNKS:
        gate = jnp.dot(hb, wi_ref[:, c0:c0 + cw], preferred_element_type=F32)
        up = jnp.dot(hb, wi_ref[:, D_FF + c0:D_FF + c0 + cw], preferred_element_type=F32)
        part = jnp.dot((_silu(gate) * up).astype(BF16), wo_ref[c0:c0 + cw, :], preferred_element_type=F32)
        y = part if y is None else y + part
    out = x + 0.5 * mod(g_ref) * y
    if final:
        out = _rms(out) * fg_ref[...]
    o_ref[...] = out


def _resident(shape, index_map):
    return pl.BlockSpec(shape, index_map, pipeline_mode=pl.Buffered(1))


def _ffn(x, rows, l, jm, w_in_b, w_out_b, final_g=None, mixer=None):
    final = final_g is not None
    in_specs = [rows.row_spec(D_MODEL), rows.mod_spec(jm), rows.mod_spec(jm + 1), rows.mod_spec(jm + 2),
                _resident((None, D_MODEL, 2 * D_FF), lambda i: (l, 0, 0)),
                _resident((None, D_FF, D_MODEL), lambda i: (l, 0, 0))]
    args = [x, rows.mod, rows.mod, rows.mod, w_in_b, w_out_b]
    if mixer is not None:
        yab, yc, w_mix = mixer
        in_specs += [rows.mod_spec(5), rows.row_spec(D_AB), rows.row_spec(D_C),
                     _resident((None, D_AB, D_MODEL), lambda i: (l, 0, 0)),
                     _resident((None, D_C, D_MODEL), lambda i: (l, D_AB // D_C, 0))]
        args += [rows.mod, yab, yc, w_mix, w_mix]
    if final:
        in_specs.append(pl.BlockSpec((1, D_MODEL), lambda i: (0, 0)))
        args.append(final_g.reshape(1, D_MODEL))
    return pl.pallas_call(
        functools.partial(_ffn_kernel, final=final, mixer_out=mixer is not None,
                          tiles_per_seq=rows.tiles_per_seq),
        grid=(rows.n_tiles,),
        in_specs=in_specs,
        out_specs=rows.row_spec(D_MODEL),
        out_shape=jax.ShapeDtypeStruct((rows.n_rows, D_MODEL), F32),
        compiler_params=_cparams(("parallel",)),
        name="ffn_final" if final else "ffn",
    )(*args)


def _proj_in_kernel(x_ref, sh_ref, sc_ref, w_ref, z_ref, *, tiles_per_seq):
    sh, sc = _mod_row(sh_ref, tiles_per_seq), _mod_row(sc_ref, tiles_per_seq)
    h = (_rms(x_ref[...]) * (1.0 + sc) + sh).astype(BF16)
    z_ref[...] = jnp.dot(h, w_ref[...], preferred_element_type=F32)


def _proj_in(x, rows, l, w_in_b):
    return pl.pallas_call(
        functools.partial(_proj_in_kernel, tiles_per_seq=rows.tiles_per_seq),
        grid=(rows.n_tiles,),
        in_specs=[rows.row_spec(D_MODEL), rows.mod_spec(3), rows.mod_spec(4),
                  _resident((None, D_MODEL, D_IN), lambda i: (l, 0, 0))],
        out_specs=rows.row_spec(D_IN),
        out_shape=jax.ShapeDtypeStruct((rows.n_rows, D_IN), F32),
        compiler_params=_cparams(("parallel",)),
        name="proj_in",
    )(x, rows.mod, rows.mod, w_in_b)


def _retention_log_decay():
    return jnp.log1p(-jnp.exp(jnp.linspace(math.log(1.0 / 32.0), math.log(1.0 / 512.0), N_HEADS, dtype=F32)))


def _rope_tables(pos):
    half = HEAD_DIM // 2
    inv = ROPE_BASE ** (-jnp.arange(half, dtype=F32) / half)
    ang = pos[:, None] * inv[None, :]
    cos, sin = jnp.cos(ang), jnp.sin(ang)
    cos_t = jnp.tile(jnp.concatenate([cos, cos], axis=-1), (1, N_HEADS))
    sin_t = jnp.tile(jnp.concatenate([-sin, sin], axis=-1), (1, N_HEADS))
    return cos_t, sin_t


def _retention_chunk_tables(c):
    lg = _retention_log_decay()
    idx = jnp.arange(c, dtype=F32)
    diff = idx[:, None] - idx[None, :]
    dmask = jnp.where(diff[None] >= 0, jnp.exp(jnp.maximum(diff, 0.0)[None] * lg[:, None, None]), 0.0)
    q_dec = jnp.exp((idx[:, None] + 1.0) * lg[None, :])
    k_dec = jnp.exp((c - 1.0 - idx)[:, None] * lg[None, :])
    s_dec = jnp.exp(c * lg)
    qdec_t = jnp.repeat(q_dec, DV_B, axis=1)
    kdec_t = jnp.repeat(k_dec, HEAD_DIM, axis=1)
    sdec_t = jnp.broadcast_to(jnp.repeat(s_dec, HEAD_DIM)[:, None], (N_HEADS * HEAD_DIM, DV_B))
    return dmask, qdec_t, kdec_t, sdec_t


def _mix_ab_kernel(z_ref, cos_ref, sin_ref, ws_ref, bs_ref, gain_ref, dmask_ref, qdec_ref, kdec_ref,
                   sdec_ref, gng_ref, yab_ref, s_ref):
    @pl.when(pl.program_id(1) == 0)
    def _():
        s_ref[...] = jnp.zeros_like(s_ref)

    seqs = range(z_ref.shape[0])
    mean_sel = _group_sel(D_A, D_A, HEAD_DIM, HEAD_DIM, 1.0 / HEAD_DIM)
    gv = [_gelu(z_ref[i, :, D_A:2 * D_A]) for i in seqs]
    vn = [g * lax.rsqrt(_dot_sel(g * g, mean_sel) + NORM_EPS) * gain_ref[...] for g in gv]
    lane_head = _head_of((CHUNK, D_A), 1, HEAD_DIM)
    row = lax.broadcasted_iota(jnp.int32, (CHUNK, CHUNK), 0)
    col = lax.broadcasted_iota(jnp.int32, (CHUNK, CHUNK), 1)
    mixed = [bs_ref[...] for _ in seqs]
    for h in range(N_HEADS):
        w_m = jnp.where(row >= col, ws_ref[h], 0.0).astype(BF16)
        mixed = [m + _dot(w_m, jnp.where(lane_head == h, v, 0.0)) for m, v in zip(mixed, vn)]
    for i in seqs:
        yab_ref[i, :, 0:D_A] = (_gelu(z_ref[i, :, 0:D_A]) * mixed[i]).astype(BF16)

    q = [_rotate_half(z_ref[i, :, 2 * D_A:3 * D_A], cos_ref[...], sin_ref[...]).astype(BF16) for i in seqs]
    k = [_rotate_half(z_ref[i, :, 3 * D_A:4 * D_A], cos_ref[...], sin_ref[...]) * (HEAD_DIM ** -0.5)
         for i in seqs]
    s_all = [s_ref[i] for i in seqs]
    kd_t = [(k_i * kdec_ref[...]).T for k_i in k]
    kb = [k_i.astype(BF16) for k_i in k]
    sb = [s.astype(BF16) for s in s_all]
    for h in range(N_HEADS):
        cols_v = slice(4 * D_A + h * DV_B, 4 * D_A + (h + 1) * DV_B)
        cols_g = slice(4 * D_A + D_BV + h * DV_B, 4 * D_A + D_BV + (h + 1) * DV_B)
        cols_h = slice(h * DV_B, (h + 1) * DV_B)
        rows_h = slice(h * HEAD_DIM, (h + 1) * HEAD_DIM)
        v_h = [z_ref[i, :, cols_v] for i in seqs]
        q_h = [jnp.where(lane_head == h, q_i, jnp.zeros_like(q_i)) for q_i in q]
        scores = [_dot_t(q_i, k_i) * dmask_ref[h] for q_i, k_i in zip(q_h, kb)]
        inner = [_dot(sc, v_i) for sc, v_i in zip(scores, v_h)]
        cross = [jnp.dot(q_i, s_i, preferred_element_type=F32) * qdec_ref[:, cols_h]
                 for q_i, s_i in zip(q_h, sb)]
        upd = [_dot(kd_i[rows_h, :], v_i) for kd_i, v_i in zip(kd_t, v_h)]
        for i in seqs:
            yb = _rms(inner[i] + cross[i]) * gng_ref[:, cols_h] * _silu(z_ref[i, :, cols_g])
            yab_ref[i, :, D_A + h * DV_B:D_A + (h + 1) * DV_B] = yb.astype(BF16)
            s_ref[i, rows_h, :] = s_all[i][rows_h, :] * sdec_ref[rows_h, :] + upd[i]


def _mix_ab_prompt(z, bsz, t, l, p):
    n_c = t // CHUNK
    pos = jnp.arange(t, dtype=F32)
    cos_t, sin_t = _rope_tables(pos)
    dmask, qdec_t, kdec_t, sdec_t = _retention_chunk_tables(CHUNK)
    bs_t = jnp.repeat(p['sg_b_s'][l].T, HEAD_DIM, axis=1)
    gain = p['sg_v_gain'][l].reshape(1, D_A)
    gng = p['ret_gn_g'][l].reshape(1, D_BV)
    const = lambda shape: pl.BlockSpec(shape, lambda b, c: (0,) * len(shape))
    nb = math.gcd(bsz, AB_SEQS_PER_STEP)
    yab, s_ret = pl.pallas_call(
        _mix_ab_kernel,
        grid=(bsz // nb, n_c),
        in_specs=[pl.BlockSpec((nb, CHUNK, Z_AB), lambda b, c: (b, c, 0)),
                  pl.BlockSpec((CHUNK, D_A), lambda b, c: (c, 0)),
                  pl.BlockSpec((CHUNK, D_A), lambda b, c: (c, 0)),
                  pl.BlockSpec((None, N_HEADS, CHUNK, CHUNK), lambda b, c: (l, 0, 0, 0)),
                  const((CHUNK, D_A)), const((1, D_A)), const((N_HEADS, CHUNK, CHUNK)),
                  const((CHUNK, D_BV)), const((CHUNK, D_A)), const((D_A, DV_B)), const((1, D_BV))],
        out_specs=[pl.BlockSpec((nb, CHUNK, D_AB), lambda b, c: (b, c, 0)),
                   pl.BlockSpec((nb, D_A, DV_B), lambda b, c: (b, 0, 0))],
        out_shape=[jax.ShapeDtypeStruct((bsz, t, D_AB), BF16),
                   jax.ShapeDtypeStruct((bsz, D_A, DV_B), F32)],
        compiler_params=_cparams(("parallel", "arbitrary")),
        name="mix_ab",
    )(z.reshape(bsz, t, D_IN), cos_t, sin_t, p['sg_w_s'], bs_t, gain, dmask, qdec_t, kdec_t, sdec_t, gng)
    return yab.reshape(bsz * t, D_AB), s_ret.reshape(bsz, N_HEADS, HEAD_DIM, DV_B)


def _softplus(x):
    return jnp.maximum(x, 0.0) + jnp.log(1.0 + jnp.exp(-jnp.abs(x)))


def _rwkv_vectors(zc, shifted, mu, w0, w2p, a0, a2p, g2, k_k, k_a):
    zs = zc + (shifted - zc) * mu
    r = zs[:, 0:D_C]
    k = zs[:, D_C:2 * D_C]
    v = zs[:, 2 * D_C:3 * D_C]
    xwa = zs[:, 3 * D_C:3 * D_C + 128]
    xg = zs[:, 3 * D_C + 128:C_IN]
    w_log = -_softplus(-(w0 + _dot(jnp.tanh(xwa), w2p))) - 0.5
    logw = -jnp.exp(w_log)
    a = jax.nn.sigmoid(a0 + _dot(xwa, a2p))
    g = _dot(jax.nn.sigmoid(xg), g2)
    kk = k * k_k
    sum_sel = _group_sel(D_C, D_C, HEAD_DIM, HEAD_DIM)
    kk = kk * lax.rsqrt(jnp.maximum(_dot_sel(kk * kk, sum_sel), 1e-24))
    k2 = k * (1.0 + (a - 1.0) * k_a)
    return r, k2, v, logw, a, g, kk


def _rwkv_output(y, r, k2, v, g, r_k, ln_g, ln_b):
    mean_sel = _group_sel(D_C, D_C, HEAD_DIM, HEAD_DIM, 1.0 / HEAD_DIM)
    sum_sel = _group_sel(D_C, D_C, HEAD_DIM, HEAD_DIM)
    yc = y - _dot_sel(y, mean_sel)
    var = _dot_sel(yc * yc, mean_sel)
    yn = yc * lax.rsqrt(var + GN_EPS) * ln_g + ln_b
    bonus = _dot_sel(r * k2 * r_k, sum_sel) * v
    return (yn + bonus) * g


def _rwkv_kernel(zc_ref, mu_ref, w0_ref, w2p_ref, a0_ref, a2p_ref, g2_ref, kk_ref, ka_ref, rk_ref,
                 lng_ref, lnb_ref, yc_ref, s_ref, prev_ref):
    @pl.when(pl.program_id(1) == 0)
    def _():
        s_ref[...] = jnp.zeros_like(s_ref)
        prev_ref[...] = jnp.zeros_like(prev_ref)

    params = [ref[...] for ref in (mu_ref, w0_ref, w2p_ref, a0_ref, a2p_ref, g2_ref, kk_ref, ka_ref,
                                   rk_ref, lng_ref, lnb_ref)]
    n_seq = zc_ref.shape[0]
    outs, s_news, lasts = _rwkv_chunks([zc_ref[i] for i in range(n_seq)],
                                       [prev_ref[i] for i in range(n_seq)],
                                       [s_ref[i] for i in range(n_seq)], params)
    for i in range(n_seq):
        yc_ref[i] = outs[i]
        s_ref[i] = s_news[i]
        prev_ref[i] = lasts[i]


def _each(fn, *lists):
    return [fn(*args) for args in zip(*lists)]


def _rwkv_chunks(zcs, prevs, s_bds, params):
    mu, w0, w2p, a0, a2p, g2, k_k, k_a, r_k, ln_g, ln_b = params
    c_len = RW_CHUNK
    nw = N_HEADS * c_len
    row_id = lax.broadcasted_iota(jnp.int32, zcs[0].shape, 0)
    tri = jnp.where(lax.broadcasted_iota(jnp.int32, (c_len, c_len), 0)
                    >= lax.broadcasted_iota(jnp.int32, (c_len, c_len), 1), 1.0, 0.0).astype(BF16)
    t_id = lax.broadcasted_iota(jnp.int32, (c_len, nw), 0)
    s_id = lax.broadcasted_iota(jnp.int32, (c_len, nw), 1) % c_len
    strict = s_id < t_id
    incl = s_id <= t_id
    bd = _head_of((nw, nw), 0, c_len) == _head_of((nw, nw), 1, c_len)

    n_seq = len(zcs)
    per_seq = lambda x: [x[i * c_len:(i + 1) * c_len] for i in range(n_seq)]
    shifted = _each(lambda zc, prev: jnp.where(row_id == 0, prev, pltpu.roll(zc, 1, axis=0)), zcs, prevs)
    vec_all = _rwkv_vectors(jnp.concatenate(zcs, axis=0), jnp.concatenate(shifted, axis=0),
                            mu, w0, w2p, a0, a2p, g2, k_k, k_a)
    r_all, k2_all, v_all, _, _, g_all, _ = vec_all
    r, k2, v, logw, a, g, kk = [per_seq(x) for x in vec_all]
    b = _each(lambda kk_i, a_i: kk_i * a_i, kk, a)

    def cumsum(lw):
        hi, mid, lo = _split3(lw)
        return (jnp.dot(tri, hi, preferred_element_type=F32) + jnp.dot(tri, mid, preferred_element_type=F32)
                + jnp.dot(tri, lo, preferred_element_type=F32))
    cum = _each(cumsum, logw)
    c_last = [c[c_len - 1:c_len, :] for c in cum]
    a_hat = _each(lambda kk_i, c, lw: -kk_i * jnp.exp(c - lw), kk, cum, logw)
    r_hat = _each(lambda r_i, c: r_i * jnp.exp(c), r, cum)
    def stacked_parts(x):
        return tuple(_stack_heads(part, HEAD_DIM) for part in _split2(x))

    def bk_parts(b_i, k_i, c):
        e_inv = jnp.exp(-c)
        return tuple(jnp.concatenate([bp, kp], axis=0)
                     for bp, kp in zip(stacked_parts(b_i * e_inv), stacked_parts(k_i * e_inv)))
    bk_t = _each(bk_parts, b, k2, cum)
    ar = _each(lambda ah, rh: _split2(jnp.concatenate([ah, rh], axis=0)), a_hat, r_hat)
    gram = _each(_dot3_t, ar, bk_t)
    x_w = [jnp.where(strict, gm[0:c_len, 0:nw], 0.0) for gm in gram]
    l_ak = [jnp.where(strict, gm[0:c_len, nw:2 * nw], 0.0) for gm in gram]
    m_rb = [jnp.where(incl, gm[c_len:2 * c_len, 0:nw], 0.0) for gm in gram]
    m_rk = [jnp.where(incl, gm[c_len:2 * c_len, nw:2 * nw], 0.0) for gm in gram]

    v_bd = [stacked_parts(v_i) for v_i in v]
    ar_s = _each(_dot3_t, ar, s_bds)
    lm_v = _each(lambda l, m, vb: _dot3(jnp.concatenate([l, m], axis=0), vb), l_ak, m_rk, v_bd)
    w_rhs = _each(lambda s_part, v_part: s_part[0:c_len] + v_part[0:c_len], ar_s, lm_v)

    def expand(m_w):
        return jnp.where(bd, jnp.concatenate([m_w] * N_HEADS, axis=0), jnp.zeros((), m_w.dtype))
    power = [x.astype(BF16) for x in x_w]
    t_inv = [jnp.where(s_id == t_id, 1.0, x) for x in x_w]
    for _ in range(5):
        power = [jnp.dot(pw, expand(pw), preferred_element_type=F32).astype(BF16) for pw in power]
        t_inv = _each(lambda t, pw: t + jnp.dot(pw, expand(t.astype(BF16)), preferred_element_type=F32),
                      t_inv, power)
    t_inv = [t.astype(BF16) for t in t_inv]
    u = _each(lambda t, w: jnp.dot(t, _stack_heads(w.astype(BF16), HEAD_DIM), preferred_element_type=F32),
              t_inv, w_rhs)
    resid = _each(lambda w, u_i, x: w - u_i + _dot3(x, stacked_parts(u_i)), w_rhs, u, x_w)
    u = _each(lambda u_i, t, rs: u_i + jnp.dot(t, _stack_heads(rs.astype(BF16), HEAD_DIM),
                                               preferred_element_type=F32), u, t_inv, resid)
    u_bd = [stacked_parts(u_i) for u_i in u]

    y = _each(lambda s_part, v_part, mb, ub: s_part[c_len:2 * c_len] + v_part[c_len:2 * c_len] + _dot3(mb, ub),
              ar_s, lm_v, m_rb, u_bd)
    outs = per_seq(_rwkv_output(jnp.concatenate(y, axis=0), r_all, k2_all, v_all, g_all,
                                r_k, ln_g, ln_b).astype(BF16))

    bd_s = _head_of(s_bds[0].shape, 0, HEAD_DIM) == _head_of(s_bds[0].shape, 1, HEAD_DIM)

    def new_state(s, u_i, v_i, b_i, k_i, c, cl):
        e_tail = jnp.exp(cl - c)
        uv_t = jnp.concatenate([u_i, v_i], axis=0).T
        s_upd = _dot3(uv_t, jnp.concatenate([b_i * e_tail, k_i * e_tail], axis=0))
        return s * jnp.exp(cl) + jnp.where(bd_s, s_upd, 0.0)
    s_news = _each(new_state, s_bds, u, v, b, k2, cum, c_last)
    lasts = [zc[c_len - 1:c_len, :] for zc in zcs]
    return outs, s_news, lasts


def _rwkv_params(l, p):
    zeros = jnp.zeros((HEAD_DIM, D_C), F32)
    w2p = jnp.concatenate([p['rw_w2'][l], zeros], axis=0).astype(BF16)
    a2p = jnp.concatenate([zeros, p['rw_a2'][l]], axis=0).astype(BF16)
    row = lambda a: a.reshape(1, -1)
    return [row(p['rw_mu'][l]), row(p['rw_w0'][l]), w2p, row(p['rw_a0'][l]), a2p,
            p['rw_g2'][l].astype(BF16), row(p['rw_k_k'][l]), row(p['rw_k_a'][l])]


def _rwkv_prompt(z, bsz, t, l, p):
    n_c = t // RW_CHUNK
    params = _rwkv_params(l, p) + [p['rw_r_k'][l].reshape(1, D_C), p['rw_ln_g'][l].reshape(1, D_C),
                                   p['rw_ln_b'][l].reshape(1, D_C)]
    const = lambda a: pl.BlockSpec(a.shape, lambda b, c: (0,) * a.ndim)
    nb = math.gcd(bsz, RW_SEQS_PER_STEP)
    yc, s_bd = pl.pallas_call(
        _rwkv_kernel,
        grid=(bsz // nb, n_c),
        in_specs=[pl.BlockSpec((nb, RW_CHUNK, C_IN), lambda b, c: (b, c, Z_AB // C_IN))]
                 + [const(a) for a in params],
        out_specs=[pl.BlockSpec((nb, RW_CHUNK, D_C), lambda b, c: (b, c, 0)),
                   pl.BlockSpec((nb, D_C, D_C), lambda b, c: (b, 0, 0))],
        out_shape=[jax.ShapeDtypeStruct((bsz, t, D_C), BF16),
                   jax.ShapeDtypeStruct((bsz, D_C, D_C), F32)],
        scratch_shapes=[pltpu.VMEM((nb, 1, C_IN), F32)],
        compiler_params=_cparams(("parallel", "arbitrary")),
        name="rwkv_chunk",
    )(z.reshape(bsz, t, D_IN), *params)
    yc = yc.reshape(bsz * t, D_C)
    s_heads = jnp.stack([s_bd[:, h * HEAD_DIM:(h + 1) * HEAD_DIM, h * HEAD_DIM:(h + 1) * HEAD_DIM]
                         for h in range(N_HEADS)], axis=1)
    return yc, s_heads


SAMPLE_SLOTS = 8
SAMPLE_WKV_VBLOCK = 16


def _samp_prep_kernel(z_ref, prev_ref, cos_ref, sin_ref, ws0_ref, bs0_ref, gain_ref, mu_ref, w0_ref,
                      w2p_ref, a0_ref, a2p_ref, g2_ref, kk_ref, ka_ref, vec_ref, qk_ref, vt_ref, v_t_ref):
    gu = _gelu(z_ref[:, 0:D_A])
    gv = _gelu(z_ref[:, D_A:2 * D_A])
    mean_sel = _group_sel(D_A, D_A, HEAD_DIM, HEAD_DIM, 1.0 / HEAD_DIM)
    vn = gv * lax.rsqrt(_dot_sel(gv * gv, mean_sel) + NORM_EPS) * gain_ref[...]
    vec_ref[0] = gu * (ws0_ref[...] * vn + bs0_ref[...])
    vec_ref[1] = vn
    q = _rotate_half(z_ref[:, 2 * D_A:3 * D_A], cos_ref[...], sin_ref[...])
    k = _rotate_half(z_ref[:, 3 * D_A:4 * D_A], cos_ref[...], sin_ref[...]) * (HEAD_DIM ** -0.5)
    vec_ref[2] = q
    vec_ref[3] = k
    qk_ref[...] = _dot_sel(q * k, _group_sel(D_A, D_BV, HEAD_DIM, DV_B))
    r, k2, v, logw, a, g, kk = _rwkv_vectors(z_ref[:, Z_AB:D_IN], prev_ref[...], mu_ref[...], w0_ref[...],
                                             w2p_ref[...], a0_ref[...], a2p_ref[...], g2_ref[...],
                                             kk_ref[...], ka_ref[...])
    vec_ref[4] = r
    vec_ref[5] = k2
    vec_ref[6] = v
    vec_ref[7] = g
    for i, x in enumerate((jnp.exp(logw), kk, kk * a, k2, r)):
        vt_ref[i] = x.T
    v_t_ref[...] = v.T


def _samp_wkv_kernel(s_ref, vt_ref, v_ref, so_ref, y_ref):
    w, kk, b, k2, r = [vt_ref[i] for i in range(5)]
    for j in range(s_ref.shape[0]):
        s = s_ref[j]
        sa = jnp.sum(s * (-kk), axis=0, keepdims=True)
        s = s * w + sa * b + v_ref[j:j + 1, :] * k2
        so_ref[j] = s
        y_ref[j:j + 1, :] = jnp.sum(s * r, axis=0, keepdims=True)


def _samp_state_kernel(sret_ref, rows_ref, vr_ref, sdec_ref, sret_o_ref, cross_ref):
    q, k = [rows_ref[:, :, i:i + 1, :] for i in range(2)]
    eye = (lax.broadcasted_iota(jnp.int32, (HEAD_DIM, HEAD_DIM), 0)
           == lax.broadcasted_iota(jnp.int32, (HEAD_DIM, HEAD_DIM), 1))

    def to_col(row):
        return jnp.sum(jnp.where(eye, row, 0.0), axis=3, keepdims=True)

    s = sret_ref[...]
    cross_ref[...] = jnp.sum(s * to_col(q), axis=2, keepdims=True)
    sret_o_ref[...] = s * sdec_ref[...][None] + to_col(k) * vr_ref[...]


def _samp_post_kernel(vec_ref, qk_ref, cross_ref, y_t_ref, zv_ref, zg_ref, qdec_ref, gng_ref, rk_ref,
                      lng_ref, lnb_ref, yab_ref, yc_ref):
    yab_ref[:, 0:D_A] = vec_ref[0].astype(BF16)
    ob = qk_ref[...] * zv_ref[...] + cross_ref[...] * qdec_ref[...]
    for h in range(N_HEADS):
        cols = slice(h * DV_B, (h + 1) * DV_B)
        yb = _rms(ob[:, cols]) * gng_ref[:, cols] * _silu(zg_ref[:, cols])
        yab_ref[:, D_A + h * DV_B:D_A + (h + 1) * DV_B] = yb.astype(BF16)
    out = _rwkv_output(y_t_ref[...].T, vec_ref[4], vec_ref[5], vec_ref[6], vec_ref[7], rk_ref[...],
                       lng_ref[...], lnb_ref[...])
    yc_ref[...] = out.astype(BF16)


def _mix_sample(z, s_ret_all, s_wkv_t_all, shift, l, p):
    bs = z.shape[0]
    bb = 8
    vb = SAMPLE_WKV_VBLOCK
    full = lambda a: pl.BlockSpec(a.shape, lambda *_: (0,) * a.ndim)
    cos_t, sin_t = _rope_tables(jnp.full((1,), PAST_LEN, F32))
    lg = _retention_log_decay()
    ws0 = jnp.repeat(p['sg_w_s'][l][:, 0, 0], HEAD_DIM).reshape(1, D_A)
    bs0 = jnp.repeat(p['sg_b_s'][l][:, 0], HEAD_DIM).reshape(1, D_A)
    gain = p['sg_v_gain'][l].reshape(1, D_A)
    prep_args = [z, shift, cos_t, sin_t, ws0, bs0, gain] + _rwkv_params(l, p)
    prep_shapes = [(SAMPLE_SLOTS, bs, D_A), (bs, D_BV), (5, D_C, bs), (D_C, bs)]
    vec, qk, vt, v_t = pl.pallas_call(
        _samp_prep_kernel,
        grid=(1,),
        in_specs=[full(a) for a in prep_args],
        out_specs=[pl.BlockSpec(s, lambda i, n=len(s): (0,) * n) for s in prep_shapes],
        out_shape=[jax.ShapeDtypeStruct(s, F32) for s in prep_shapes],
        compiler_params=_cparams(("arbitrary",)),
        name="sample_prep",
    )(*prep_args)

    s_wkv_new_t, y_t = pl.pallas_call(
        _samp_wkv_kernel,
        grid=(N_HEADS, HEAD_DIM // vb),
        in_specs=[pl.BlockSpec((None, None, vb, HEAD_DIM, bs), lambda h, i: (l, h, i, 0, 0)),
                  pl.BlockSpec((5, None, HEAD_DIM, bs), lambda h, i: (0, h, 0, 0)),
                  pl.BlockSpec((None, vb, bs), lambda h, i: (h, i, 0))],
        out_specs=[pl.BlockSpec((None, vb, HEAD_DIM, bs), lambda h, i: (h, i, 0, 0)),
                   pl.BlockSpec((None, vb, bs), lambda h, i: (h, i, 0))],
        out_shape=[jax.ShapeDtypeStruct((N_HEADS, HEAD_DIM, HEAD_DIM, bs), F32),
                   jax.ShapeDtypeStruct((N_HEADS, HEAD_DIM, bs), F32)],
        compiler_params=_cparams(("parallel", "parallel")),
        name="sample_wkv",
    )(s_wkv_t_all, vt.reshape(5, N_HEADS, HEAD_DIM, bs), v_t.reshape(N_HEADS, HEAD_DIM, bs))

    zv = z[:, 4 * D_A:4 * D_A + D_BV]
    zg = z[:, 4 * D_A + D_BV:Z_AB]
    sdec_t = jnp.broadcast_to(jnp.exp(lg)[:, None, None], (N_HEADS, HEAD_DIM, DV_B))
    rows = jnp.transpose(vec[2:4].reshape(2, bs, N_HEADS, HEAD_DIM), (1, 2, 0, 3))
    blk = lambda shape: pl.BlockSpec((bb,) + shape[1:], lambda i: (i, 0, 0, 0))
    out_shapes = [jax.ShapeDtypeStruct((bs, N_HEADS, HEAD_DIM, DV_B), F32),
                  jax.ShapeDtypeStruct((bs, N_HEADS, 1, DV_B), F32)]
    s_ret_new, cross = pl.pallas_call(
        _samp_state_kernel,
        grid=(bs // bb,),
        in_specs=[pl.BlockSpec((None, bb) + s_ret_all.shape[2:], lambda i: (l, i, 0, 0, 0)),
                  blk(rows.shape), blk((bs, N_HEADS, 1, DV_B)),
                  pl.BlockSpec(sdec_t.shape, lambda i: (0, 0, 0))],
        out_specs=[blk(a.shape) for a in out_shapes],
        out_shape=out_shapes,
        compiler_params=_cparams(("parallel",)),
        name="sample_state",
    )(s_ret_all, rows, zv.reshape(bs, N_HEADS, 1, DV_B), sdec_t)

    qdec_t = jnp.repeat(jnp.exp(lg), DV_B).reshape(1, D_BV)
    post_args = [vec, qk, cross.reshape(bs, D_BV), y_t.reshape(D_C, bs), zv, zg, qdec_t,
                 p['ret_gn_g'][l].reshape(1, D_BV), p['rw_r_k'][l].reshape(1, D_C),
                 p['rw_ln_g'][l].reshape(1, D_C), p['rw_ln_b'][l].reshape(1, D_C)]
    yab, yc = pl.pallas_call(
        _samp_post_kernel,
        grid=(1,),
        in_specs=[full(a) for a in post_args],
        out_specs=[pl.BlockSpec((bs, D_AB), lambda i: (0, 0)), pl.BlockSpec((bs, D_C), lambda i: (0, 0))],
        out_shape=[jax.ShapeDtypeStruct((bs, D_AB), BF16), jax.ShapeDtypeStruct((bs, D_C), BF16)],
        compiler_params=_cparams(("arbitrary",)),
        name="sample_post",
    )(*post_args)
    return yab, yc, s_ret_new, s_wkv_new_t, vec[1]


def _row_tile(n_rows_per_seq):
    for t in (512, 256, 128):
        if n_rows_per_seq % t == 0:
            return t
    raise ValueError("sequence length must be a multiple of 128")


def kernel(x_prompt, x_sample, state_ret, state_wkv, state_shift, c_prompt, c_sample,
           w_ada, b_ada, w_ffn1_in, w_ffn1_out, w_in, w_out, w_ffn2_in, w_ffn2_out,
           sg_v_gain, sg_w_s, sg_b_s, ret_gn_g,
           rw_mu, rw_w0, rw_w2, rw_a0, rw_a2, rw_g2, rw_k_k, rw_k_a, rw_r_k, rw_ln_g, rw_ln_b,
           final_g):
    p = dict(sg_v_gain=sg_v_gain, sg_w_s=sg_w_s, sg_b_s=sg_b_s, ret_gn_g=ret_gn_g, rw_mu=rw_mu,
             rw_w0=rw_w0, rw_w2=rw_w2, rw_a0=rw_a0, rw_a2=rw_a2, rw_g2=rw_g2, rw_k_k=rw_k_k,
             rw_k_a=rw_k_a, rw_r_k=rw_r_k, rw_ln_g=rw_ln_g, rw_ln_b=rw_ln_b, final_g=final_g)
    wb = dict(ffn1_in=w_ffn1_in.astype(BF16), ffn1_out=w_ffn1_out.astype(BF16), w_in=w_in.astype(BF16),
              w_out=w_out.astype(BF16), ffn2_in=w_ffn2_in.astype(BF16), ffn2_out=w_ffn2_out.astype(BF16))
    bp, t, _ = x_prompt.shape
    bs = x_sample.shape[0]
    assert x_sample.shape[1] == 1 and t % CHUNK == 0

    bp_pad = -(-bp // SUBLANES) * SUBLANES
    assert bs % bp_pad == 0
    c_all = jnp.concatenate([c_sample, c_prompt, jnp.zeros((bp_pad - bp, D_MODEL), F32)], axis=0)
    mod = _ada(c_all, w_ada, b_ada)

    def prompt_mixer(z, l):
        yab, s_ret = _mix_ab_prompt(z, bp, t, l, p)
        yc, s_wkv = _rwkv_prompt(z, bp, t, l, p)
        last = z.reshape(bp, t, D_IN)[:, t - 1, Z_AB:]
        return yab, yc, (s_ret, s_wkv, last)

    tile_p = _row_tile(t)
    def run(x, make_rows, mixer):
        extras = []
        for l in range(DEPTH):
            rows = make_rows(l)
            x = _ffn(x, rows, l, 0, wb['ffn1_in'], wb['ffn1_out'])
            z = _proj_in(x, rows, l, wb['w_in'])
            yab, yc, extra = mixer(z, l)
            x = _ffn(x, rows, l, 6, wb['ffn2_in'], wb['ffn2_out'],
                     final_g=final_g if l == DEPTH - 1 else None, mixer=(yab, yc, wb['w_out']))
            extras.append(extra)
        return x, extras

    rows_p = lambda l: _Rows(mod, l, bp * t, tile_p, tiles_per_seq=t // tile_p, n_seq=bp_pad,
                             seq_block=bs // bp_pad)
    y_p, ex_p = run(x_prompt.reshape(bp * t, D_MODEL), rows_p, prompt_mixer)

    state_wkv_t = jnp.transpose(state_wkv, (0, 2, 3, 4, 1))

    def sample_mixer(z, l):
        yab, yc, s_ret, s_wkv_t, vn = _mix_sample(z, state_ret, state_wkv_t, state_shift[l], l, p)
        return yab, yc, (s_ret, s_wkv_t, z[:, Z_AB:], vn)

    rows_s = lambda l: _Rows(mod, l, bs, bs)
    y_s, ex_s = run(x_sample.reshape(bs, D_MODEL), rows_s, sample_mixer)

    stack = lambda ex, i: jnp.stack([e[i] for e in ex])
    return (y_p.reshape(bp, t, D_MODEL), y_s.reshape(bs, 1, D_MODEL),
            stack(ex_p, 0), stack(ex_p, 1), stack(ex_p, 2),
            stack(ex_s, 0), jnp.transpose(stack(ex_s, 1), (0, 4, 1, 2, 3)), stack(ex_s, 2),
            stack(ex_s, 3).reshape(DEPTH, bs, 1, D_A))
```

```python
import functools
import math

import jax
import jax.numpy as jnp
from jax import lax
from jax.experimental import pallas as pl
from jax.experimental.pallas import tpu as pltpu

F32 = jnp.float32
BF16 = jnp.bfloat16

D_MODEL = 1024
DEPTH = 4
N_MOD = 9
D_FF = 2816
N_HEADS = 4
HEAD_DIM = 64
DV_B = 128
CHUNK = 128
RW_CHUNK = 64
AB_SEQS_PER_STEP = 8
RW_SEQS_PER_STEP = 8
D_A = N_HEADS * HEAD_DIM
D_BV = N_HEADS * DV_B
D_C = N_HEADS * HEAD_DIM
C_IN = 1024
D_IN = 3072
Z_AB = 2048
D_AB = D_A + D_BV
ROPE_BASE = 10000.0
NORM_EPS = 1e-6
GN_EPS = 64e-5
PAST_LEN = 16384

VMEM_LIMIT_BYTES = 56 * 1024 * 1024
SUBLANES = 8
MXU_WIDTH = 256
FFN_CHUNKS = ((0, 6 * MXU_WIDTH), (6 * MXU_WIDTH, 5 * MXU_WIDTH))


def _cparams(sem):
    return pltpu.CompilerParams(dimension_semantics=sem, vmem_limit_bytes=VMEM_LIMIT_BYTES)


def _dot(a, b):
    return jnp.dot(a.astype(BF16), b.astype(BF16), preferred_element_type=F32)


def _dot_t(a, b):
    return lax.dot_general(a.astype(BF16), b.astype(BF16), (((1,), (1,)), ((), ())),
                           preferred_element_type=F32)


def _split2(x):
    hi = x.astype(BF16)
    lo = (x - hi.astype(F32)).astype(BF16)
    return hi, lo


def _split3(x):
    hi = x.astype(BF16)
    r1 = x - hi.astype(F32)
    mid = r1.astype(BF16)
    lo = (r1 - mid.astype(F32)).astype(BF16)
    return hi, mid, lo


def _dot_sel(x, sel):
    m = x.shape[0]
    out = jnp.dot(jnp.concatenate(_split2(x), axis=0), sel, preferred_element_type=F32)
    return out[0:m] + out[m:2 * m]


def _dot3_general(a, b, dn):
    ah, al = a if isinstance(a, tuple) else _split2(a)
    bh, bl = b if isinstance(b, tuple) else _split2(b)
    m = ah.shape[0]
    top = lax.dot_general(jnp.concatenate([ah, al], axis=0), bh, dn, preferred_element_type=F32)
    return top[0:m] + top[m:2 * m] + lax.dot_general(ah, bl, dn, preferred_element_type=F32)


_DN_AB = (((1,), (0,)), ((), ()))
_DN_ABT = (((1,), (1,)), ((), ()))


def _dot3_with_rider(a, rider, b, dn):
    ah, al = a if isinstance(a, tuple) else _split2(a)
    bh, bl = b if isinstance(b, tuple) else _split2(b)
    m, n = ah.shape[0], rider.shape[0]
    top = lax.dot_general(jnp.concatenate([ah, al, rider.astype(BF16)], axis=0), bh, dn,
                          preferred_element_type=F32)
    main = top[0:m] + top[m:2 * m] + lax.dot_general(ah, bl, dn, preferred_element_type=F32)
    return main, top[2 * m:2 * m + n]


def _dot3(a, b):
    return _dot3_general(a, b, _DN_AB)


def _rms(x, eps=NORM_EPS):
    return x * lax.rsqrt(jnp.mean(x * x, axis=-1, keepdims=True) + eps)


def _silu(x):
    return x * jax.nn.sigmoid(x)


def _gelu(x):
    c = math.sqrt(2.0 / math.pi)
    return (0.5 * x) * (1.0 + jnp.tanh(x * (c + (c * 0.044715) * (x * x))))


def _head_of(shape, dim, width):
    return lax.broadcasted_iota(jnp.int32, shape, dim) // width


def _group_sel(n_in, n_out, w_in, w_out, scale=1.0):
    gi = _head_of((n_in, n_out), 0, w_in)
    go = _head_of((n_in, n_out), 1, w_out)
    return jnp.where(gi == go, scale, 0.0).astype(BF16)


def _stack_heads(x, width):
    lane_head = _head_of(x.shape, 1, width)
    return jnp.concatenate([jnp.where(lane_head == h, x, 0.0) for h in range(N_HEADS)], axis=0)


def _rotate_half(x, cos_t, sin_t):
    n = x.shape[1]
    half = HEAD_DIM // 2
    lane = lax.broadcasted_iota(jnp.int32, x.shape, 1)
    fwd = pltpu.roll(x, n - half, axis=1)
    bwd = pltpu.roll(x, half, axis=1)
    partner = jnp.where((lane % HEAD_DIM) < half, fwd, bwd)
    return x * cos_t + partner * sin_t


ADA_VECS_PER_STEP = 3


def _ada_kernel(c_ref, w_ref, b_ref, o_ref):
    cs = _silu(c_ref[...]).astype(BF16)
    for j in range(ADA_VECS_PER_STEP):
        cols = slice(j * D_MODEL, (j + 1) * D_MODEL)
        o_ref[j] = _dot(cs, w_ref[:, cols]) + b_ref[:, cols]


def _ada(c_all, w_ada, b_ada):
    bt = c_all.shape[0]
    nv = ADA_VECS_PER_STEP
    return pl.pallas_call(
        _ada_kernel,
        grid=(DEPTH, N_MOD // nv),
        in_specs=[pl.BlockSpec((bt, D_MODEL), lambda l, j: (0, 0)),
                  pl.BlockSpec((None, D_MODEL, nv * D_MODEL), lambda l, j: (l, 0, j)),
                  pl.BlockSpec((None, 1, nv * D_MODEL), lambda l, j: (l, 0, j))],
        out_specs=pl.BlockSpec((None, nv, bt, D_MODEL), lambda l, j: (l, j, 0, 0)),
        out_shape=jax.ShapeDtypeStruct((DEPTH, N_MOD, bt, D_MODEL), F32),
        compiler_params=_cparams(("parallel", "parallel")),
        name="ada_mod",
    )(c_all, w_ada, b_ada.reshape(DEPTH, 1, N_MOD * D_MODEL))


class _Rows:
    def __init__(self, mod, layer, n_rows, tile, tiles_per_seq=None, n_seq=None, seq_block=None):
        self.mod, self.layer, self.n_rows, self.tile = mod, layer, n_rows, tile
        self.n_tiles = n_rows // tile
        self.tiles_per_seq, self.n_seq, self.seq_block = tiles_per_seq, n_seq, seq_block

    def mod_spec(self, j):
        l = self.layer
        if self.tiles_per_seq is None:
            return pl.BlockSpec((None, None, self.tile, D_MODEL), lambda i: (l, j, i, 0))
        blk = self.seq_block
        return pl.BlockSpec((None, None, self.n_seq, D_MODEL), lambda i: (l, j, blk, 0))

    def row_spec(self, width):
        return pl.BlockSpec((self.tile, width), lambda i: (i, 0))


def _mod_row(ref, tiles_per_seq):
    if tiles_per_seq is None:
        return ref[...]
    return ref[pl.ds(pl.program_id(0) // tiles_per_seq, 1), :]


def _ffn_kernel(*refs, final, mixer_out, tiles_per_seq):
    refs = list(refs)
    o_ref = refs.pop()
    x_ref, sh_ref, sc_ref, g_ref, wi_ref, wo_ref = refs[:6]
    rest = refs[6:]
    mod = functools.partial(_mod_row, tiles_per_seq=tiles_per_seq)
    x = x_ref[...]
    if mixer_out:
        gm_ref, yab_ref, yc_ref, wab_ref, wc_ref = rest[:5]
        rest = rest[5:]
        x = x + mod(gm_ref) * (jnp.dot(yab_ref[...], wab_ref[...], preferred_element_type=F32)
                               + jnp.dot(yc_ref[...], wc_ref[...], preferred_element_type=F32))
    if final:
        fg_ref, = rest
    hb = (_rms(x) * (1.0 + mod(sc_ref)) + mod(sh_ref)).astype(BF16)
    y = None
    for c0, cw in FFN_CHUNKS:
        gate = jnp.dot(hb, wi_ref[:, c0:c0 + cw], preferred_element_type=F32)
        up = jnp.dot(hb, wi_ref[:, D_FF + c0:D_FF + c0 + cw], preferred_element_type=F32)
        part = jnp.dot((_silu(gate) * up).astype(BF16), wo_ref[c0:c0 + cw, :], preferred_element_type=F32)
        y = part if y is None else y + part
    out = x + 0.5 * mod(g_ref) * y
    if final:
        out = _rms(out) * fg_ref[...]
    o_ref[...] = out


def _resident(shape, index_map):
    return pl.BlockSpec(shape, index_map, pipeline_mode=pl.Buffered(1))


def _ffn(x, rows, l, jm, w_in_b, w_out_b, final_g=None, mixer=None):
    final = final_g is not None
    in_specs = [rows.row_spec(D_MODEL), rows.mod_spec(jm), rows.mod_spec(jm + 1), rows.mod_spec(jm + 2),
                _resident((None, D_MODEL, 2 * D_FF), lambda i: (l, 0, 0)),
                _resident((None, D_FF, D_MODEL), lambda i: (l, 0, 0))]
    args = [x, rows.mod, rows.mod, rows.mod, w_in_b, w_out_b]
    if mixer is not None:
        yab, yc, w_mix = mixer
        in_specs += [rows.mod_spec(5), rows.row_spec(D_AB), rows.row_spec(D_C),
                     _resident((None, D_AB, D_MODEL), lambda i: (l, 0, 0)),
                     _resident((None, D_C, D_MODEL), lambda i: (l, D_AB // D_C, 0))]
        args += [rows.mod, yab, yc, w_mix, w_mix]
    if final:
        in_specs.append(pl.BlockSpec((1, D_MODEL), lambda i: (0, 0)))
        args.append(final_g.reshape(1, D_MODEL))
    return pl.pallas_call(
        functools.partial(_ffn_kernel, final=final, mixer_out=mixer is not None,
                          tiles_per_seq=rows.tiles_per_seq),
        grid=(rows.n_tiles,),
        in_specs=in_specs,
        out_specs=rows.row_spec(D_MODEL),
        out_shape=jax.ShapeDtypeStruct((rows.n_rows, D_MODEL), F32),
        compiler_params=_cparams(("parallel",)),
        name="ffn_final" if final else "ffn",
    )(*args)


def _proj_in_kernel(x_ref, sh_ref, sc_ref, w_ref, z_ref, *, tiles_per_seq):
    sh, sc = _mod_row(sh_ref, tiles_per_seq), _mod_row(sc_ref, tiles_per_seq)
    h = (_rms(x_ref[...]) * (1.0 + sc) + sh).astype(BF16)
    z_ref[...] = jnp.dot(h, w_ref[...], preferred_element_type=F32)


def _proj_in(x, rows, l, w_in_b):
    return pl.pallas_call(
        functools.partial(_proj_in_kernel, tiles_per_seq=rows.tiles_per_seq),
        grid=(rows.n_tiles,),
        in_specs=[rows.row_spec(D_MODEL), rows.mod_spec(3), rows.mod_spec(4),
                  _resident((None, D_MODEL, D_IN), lambda i: (l, 0, 0))],
        out_specs=rows.row_spec(D_IN),
        out_shape=jax.ShapeDtypeStruct((rows.n_rows, D_IN), F32),
        compiler_params=_cparams(("parallel",)),
        name="proj_in",
    )(x, rows.mod, rows.mod, w_in_b)


def _retention_log_decay():
    return jnp.log1p(-jnp.exp(jnp.linspace(math.log(1.0 / 32.0), math.log(1.0 / 512.0), N_HEADS, dtype=F32)))


def _rope_tables(pos):
    half = HEAD_DIM // 2
    inv = ROPE_BASE ** (-jnp.arange(half, dtype=F32) / half)
    ang = pos[:, None] * inv[None, :]
    cos, sin = jnp.cos(ang), jnp.sin(ang)
    cos_t = jnp.tile(jnp.concatenate([cos, cos], axis=-1), (1, N_HEADS))
    sin_t = jnp.tile(jnp.concatenate([-sin, sin], axis=-1), (1, N_HEADS))
    return cos_t, sin_t


def _retention_chunk_tables(c):
    lg = _retention_log_decay()
    idx = jnp.arange(c, dtype=F32)
    diff = idx[:, None] - idx[None, :]
    dmask = jnp.where(diff[None] >= 0, jnp.exp(jnp.maximum(diff, 0.0)[None] * lg[:, None, None]), 0.0)
    q_dec = jnp.exp((idx[:, None] + 1.0) * lg[None, :])
    k_dec = jnp.exp((c - 1.0 - idx)[:, None] * lg[None, :])
    s_dec = jnp.exp(c * lg)
    qdec_t = jnp.repeat(q_dec, DV_B, axis=1)
    kdec_t = jnp.repeat(k_dec, HEAD_DIM, axis=1)
    sdec_t = jnp.broadcast_to(jnp.repeat(s_dec, HEAD_DIM)[:, None], (N_HEADS * HEAD_DIM, DV_B))
    return dmask, qdec_t, kdec_t, sdec_t


def _mix_ab_kernel(z_ref, cos_ref, sin_ref, ws_ref, bs_ref, gain_ref, dmask_ref, qdec_ref, kdec_ref,
                   sdec_ref, gng_ref, yab_ref, s_ref):
    @pl.when(pl.program_id(1) == 0)
    def _():
        s_ref[...] = jnp.zeros_like(s_ref)

    seqs = range(z_ref.shape[0])
    mean_sel = _group_sel(D_A, D_A, HEAD_DIM, HEAD_DIM, 1.0 / HEAD_DIM)
    gv = [_gelu(z_ref[i, :, D_A:2 * D_A]) for i in seqs]
    vn = [g * lax.rsqrt(_dot_sel(g * g, mean_sel) + NORM_EPS) * gain_ref[...] for g in gv]
    lane_head = _head_of((CHUNK, D_A), 1, HEAD_DIM)
    row = lax.broadcasted_iota(jnp.int32, (CHUNK, CHUNK), 0)
    col = lax.broadcasted_iota(jnp.int32, (CHUNK, CHUNK), 1)
    mixed = [bs_ref[...] for _ in seqs]
    for h in range(N_HEADS):
        w_m = jnp.where(row >= col, ws_ref[h], 0.0).astype(BF16)
        mixed = [m + _dot(w_m, jnp.where(lane_head == h, v, 0.0)) for m, v in zip(mixed, vn)]
    for i in seqs:
        yab_ref[i, :, 0:D_A] = (_gelu(z_ref[i, :, 0:D_A]) * mixed[i]).astype(BF16)

    q = [_rotate_half(z_ref[i, :, 2 * D_A:3 * D_A], cos_ref[...], sin_ref[...]).astype(BF16) for i in seqs]
    k = [_rotate_half(z_ref[i, :, 3 * D_A:4 * D_A], cos_ref[...], sin_ref[...]) * (HEAD_DIM ** -0.5)
         for i in seqs]
    s_all = [s_ref[i] for i in seqs]
    kd_t = [(k_i * kdec_ref[...]).T for k_i in k]
    kb = [k_i.astype(BF16) for k_i in k]
    sb = [s.astype(BF16) for s in s_all]
    for h in range(N_HEADS):
        cols_v = slice(4 * D_A + h * DV_B, 4 * D_A + (h + 1) * DV_B)
        cols_g = slice(4 * D_A + D_BV + h * DV_B, 4 * D_A + D_BV + (h + 1) * DV_B)
        cols_h = slice(h * DV_B, (h + 1) * DV_B)
        rows_h = slice(h * HEAD_DIM, (h + 1) * HEAD_DIM)
        v_h = [z_ref[i, :, cols_v] for i in seqs]
        q_h = [jnp.where(lane_head == h, q_i, jnp.zeros_like(q_i)) for q_i in q]
        scores = [_dot_t(q_i, k_i) * dmask_ref[h] for q_i, k_i in zip(q_h, kb)]
        inner = [_dot(sc, v_i) for sc, v_i in zip(scores, v_h)]
        cross = [jnp.dot(q_i, s_i, preferred_element_type=F32) * qdec_ref[:, cols_h]
                 for q_i, s_i in zip(q_h, sb)]
        upd = [_dot(kd_i[rows_h, :], v_i) for kd_i, v_i in zip(kd_t, v_h)]
        for i in seqs:
            yb = _rms(inner[i] + cross[i]) * gng_ref[:, cols_h] * _silu(z_ref[i, :, cols_g])
            yab_ref[i, :, D_A + h * DV_B:D_A + (h + 1) * DV_B] = yb.astype(BF16)
            s_ref[i, rows_h, :] = s_all[i][rows_h, :] * sdec_ref[rows_h, :] + upd[i]


def _mix_ab_prompt(z, bsz, t, l, p):
    n_c = t // CHUNK
    pos = jnp.arange(t, dtype=F32)
    cos_t, sin_t = _rope_tables(pos)
    dmask, qdec_t, kdec_t, sdec_t = _retention_chunk_tables(CHUNK)
    bs_t = jnp.repeat(p['sg_b_s'][l].T, HEAD_DIM, axis=1)
    gain = p['sg_v_gain'][l].reshape(1, D_A)
    gng = p['ret_gn_g'][l].reshape(1, D_BV)
    const = lambda shape: pl.BlockSpec(shape, lambda b, c: (0,) * len(shape))
    nb = math.gcd(bsz, AB_SEQS_PER_STEP)
    yab, s_ret = pl.pallas_call(
        _mix_ab_kernel,
        grid=(bsz // nb, n_c),
        in_specs=[pl.BlockSpec((nb, CHUNK, Z_AB), lambda b, c: (b, c, 0)),
                  pl.BlockSpec((CHUNK, D_A), lambda b, c: (c, 0)),
                  pl.BlockSpec((CHUNK, D_A), lambda b, c: (c, 0)),
                  pl.BlockSpec((None, N_HEADS, CHUNK, CHUNK), lambda b, c: (l, 0, 0, 0)),
                  const((CHUNK, D_A)), const((1, D_A)), const((N_HEADS, CHUNK, CHUNK)),
                  const((CHUNK, D_BV)), const((CHUNK, D_A)), const((D_A, DV_B)), const((1, D_BV))],
        out_specs=[pl.BlockSpec((nb, CHUNK, D_AB), lambda b, c: (b, c, 0)),
                   pl.BlockSpec((nb, D_A, DV_B), lambda b, c: (b, 0, 0))],
        out_shape=[jax.ShapeDtypeStruct((bsz, t, D_AB), BF16),
                   jax.ShapeDtypeStruct((bsz, D_A, DV_B), F32)],
        compiler_params=_cparams(("parallel", "arbitrary")),
        name="mix_ab",
    )(z.reshape(bsz, t, D_IN), cos_t, sin_t, p['sg_w_s'], bs_t, gain, dmask, qdec_t, kdec_t, sdec_t, gng)
    return yab.reshape(bsz * t, D_AB), s_ret.reshape(bsz, N_HEADS, HEAD_DIM, DV_B)


def _softplus(x):
    return jnp.maximum(x, 0.0) + jnp.log(1.0 + jnp.exp(-jnp.abs(x)))


def _rwkv_vectors(zc, shifted, mu, w0, w2p, a0, a2p, g2, k_k, k_a):
    zs = zc + (shifted - zc) * mu
    r = zs[:, 0:D_C]
    k = zs[:, D_C:2 * D_C]
    v = zs[:, 2 * D_C:3 * D_C]
    xwa = zs[:, 3 * D_C:3 * D_C + 128]
    xg = zs[:, 3 * D_C + 128:C_IN]
    w_log = -_softplus(-(w0 + _dot(jnp.tanh(xwa), w2p))) - 0.5
    logw = -jnp.exp(w_log)
    a = jax.nn.sigmoid(a0 + _dot(xwa, a2p))
    g = _dot(jax.nn.sigmoid(xg), g2)
    kk = k * k_k
    sum_sel = _group_sel(D_C, D_C, HEAD_DIM, HEAD_DIM)
    kk = kk * lax.rsqrt(jnp.maximum(_dot_sel(kk * kk, sum_sel), 1e-24))
    k2 = k * (1.0 + (a - 1.0) * k_a)
    return r, k2, v, logw, a, g, kk


def _rwkv_output(y, r, k2, v, g, r_k, ln_g, ln_b):
    mean_sel = _group_sel(D_C, D_C, HEAD_DIM, HEAD_DIM, 1.0 / HEAD_DIM)
    sum_sel = _group_sel(D_C, D_C, HEAD_DIM, HEAD_DIM)
    yc = y - _dot_sel(y, mean_sel)
    var = _dot_sel(yc * yc, mean_sel)
    yn = yc * lax.rsqrt(var + GN_EPS) * ln_g + ln_b
    bonus = _dot_sel(r * k2 * r_k, sum_sel) * v
    return (yn + bonus) * g


def _rwkv_kernel(zc_ref, mu_ref, w0_ref, w2p_ref, a0_ref, a2p_ref, g2_ref, kk_ref, ka_ref, rk_ref,
                 lng_ref, lnb_ref, yc_ref, s_ref, prev_ref):
    @pl.when(pl.program_id(1) == 0)
    def _():
        s_ref[...] = jnp.zeros_like(s_ref)
        prev_ref[...] = jnp.zeros_like(prev_ref)

    params = [ref[...] for ref in (mu_ref, w0_ref, w2p_ref, a0_ref, a2p_ref, g2_ref, kk_ref, ka_ref,
                                   rk_ref, lng_ref, lnb_ref)]
    n_seq = zc_ref.shape[0]
    outs, s_news, lasts = _rwkv_chunks([zc_ref[i] for i in range(n_seq)],
                                       [prev_ref[i] for i in range(n_seq)],
                                       [s_ref[i] for i in range(n_seq)], params)
    for i in range(n_seq):
        yc_ref[i] = outs[i]
        s_ref[i] = s_news[i]
        prev_ref[i] = lasts[i]


def _each(fn, *lists):
    return [fn(*args) for args in zip(*lists)]


def _rwkv_chunks(zcs, prevs, s_bds, params):
    mu, w0, w2p, a0, a2p, g2, k_k, k_a, r_k, ln_g, ln_b = params
    c_len = RW_CHUNK
    nw = N_HEADS * c_len
    row_id = lax.broadcasted_iota(jnp.int32, zcs[0].shape, 0)
    tri = jnp.where(lax.broadcasted_iota(jnp.int32, (c_len, c_len), 0)
                    >= lax.broadcasted_iota(jnp.int32, (c_len, c_len), 1), 1.0, 0.0).astype(BF16)
    t_id = lax.broadcasted_iota(jnp.int32, (c_len, nw), 0)
    s_id = lax.broadcasted_iota(jnp.int32, (c_len, nw), 1) % c_len
    strict = s_id < t_id
    incl = s_id <= t_id
    bd = _head_of((nw, nw), 0, c_len) == _head_of((nw, nw), 1, c_len)

    n_seq = len(zcs)
    per_seq = lambda x: [x[i * c_len:(i + 1) * c_len] for i in range(n_seq)]
    shifted = _each(lambda zc, prev: jnp.where(row_id == 0, prev, pltpu.roll(zc, 1, axis=0)), zcs, prevs)
    vec_all = _rwkv_vectors(jnp.concatenate(zcs, axis=0), jnp.concatenate(shifted, axis=0),
                            mu, w0, w2p, a0, a2p, g2, k_k, k_a)
    r_all, k2_all, v_all, _, _, g_all, _ = vec_all
    r, k2, v, logw, a, g, kk = [per_seq(x) for x in vec_all]
    b = _each(lambda kk_i, a_i: kk_i * a_i, kk, a)

    def cumsum(lw):
        hi, mid, lo = _split3(lw)
        return (jnp.dot(tri, hi, preferred_element_type=F32) + jnp.dot(tri, mid, preferred_element_type=F32)
                + jnp.dot(tri, lo, preferred_element_type=F32))
    cum = _each(cumsum, logw)
    c_last = [c[c_len - 1:c_len, :] for c in cum]
    a_hat = _each(lambda kk_i, c, lw: -kk_i * jnp.exp(c - lw), kk, cum, logw)
    r_hat = _each(lambda r_i, c: r_i * jnp.exp(c), r, cum)
    def stacked_parts(x):
        return tuple(_stack_heads(part, HEAD_DIM) for part in _split2(x))

    def bk_parts(b_i, k_i, c):
        e_inv = jnp.exp(-c)
        return tuple(jnp.concatenate([bp, kp], axis=0)
                     for bp, kp in zip(stacked_parts(b_i * e_inv), stacked_parts(k_i * e_inv)))
    bk_t = _each(bk_parts, b, k2, cum)
    a_parts = [_split2(ah) for ah in a_hat]
    gram = _each(lambda ap, rh, bk: _dot3_with_rider(ap, rh, bk, _DN_ABT), a_parts, r_hat, bk_t)
    x_w = [jnp.where(strict, ga[:, 0:nw], 0.0) for ga, _ in gram]
    l_ak = [jnp.where(strict, ga[:, nw:2 * nw], 0.0) for ga, _ in gram]
    m_rb = [jnp.where(incl, gr[:, 0:nw], 0.0) for _, gr in gram]
    m_rk = [jnp.where(incl, gr[:, nw:2 * nw], 0.0) for _, gr in gram]

    v_bd = [stacked_parts(v_i) for v_i in v]
    ar_s = _each(lambda ap, rh, s: _dot3_with_rider(ap, rh, s, _DN_ABT), a_parts, r_hat, s_bds)
    lm_v = _each(lambda l, m, vb: _dot3_with_rider(l, m, vb, _DN_AB), l_ak, m_rk, v_bd)
    w_rhs = _each(lambda s_part, v_part: s_part[0] + v_part[0], ar_s, lm_v)

    def expand(m_w):
        return jnp.where(bd, jnp.concatenate([m_w] * N_HEADS, axis=0), jnp.zeros((), m_w.dtype))
    power = [x.astype(BF16) for x in x_w]
    t_inv = [jnp.where(s_id == t_id, 1.0, x) for x in x_w]
    for _ in range(5):
        power = [jnp.dot(pw, expand(pw), preferred_element_type=F32).astype(BF16) for pw in power]
        t_inv = _each(lambda t, pw: t + jnp.dot(pw, expand(t.astype(BF16)), preferred_element_type=F32),
                      t_inv, power)
    t_inv = [t.astype(BF16) for t in t_inv]
    u = _each(lambda t, w: jnp.dot(t, _stack_heads(w.astype(BF16), HEAD_DIM), preferred_element_type=F32),
              t_inv, w_rhs)
    resid = _each(lambda w, u_i, x: w - u_i + _dot3(x, stacked_parts(u_i)), w_rhs, u, x_w)
    u = _each(lambda u_i, t, rs: u_i + jnp.dot(t, _stack_heads(rs.astype(BF16), HEAD_DIM),
                                               preferred_element_type=F32), u, t_inv, resid)
    u_bd = [stacked_parts(u_i) for u_i in u]

    y = _each(lambda s_part, v_part, mb, ub: s_part[1] + v_part[1] + jnp.dot(mb.astype(BF16), ub[0],
                                                                             preferred_element_type=F32),
              ar_s, lm_v, m_rb, u_bd)
    outs = per_seq(_rwkv_output(jnp.concatenate(y, axis=0), r_all, k2_all, v_all, g_all,
                                r_k, ln_g, ln_b).astype(BF16))

    bd_s = _head_of(s_bds[0].shape, 0, HEAD_DIM) == _head_of(s_bds[0].shape, 1, HEAD_DIM)

    def new_state(s, u_i, v_i, b_i, k_i, c, cl):
        e_tail = jnp.exp(cl - c)
        uv_t = jnp.concatenate([u_i, v_i], axis=0).T
        s_upd = _dot3(uv_t, jnp.concatenate([b_i * e_tail, k_i * e_tail], axis=0))
        return s * jnp.exp(cl) + jnp.where(bd_s, s_upd, 0.0)
    s_news = _each(new_state, s_bds, u, v, b, k2, cum, c_last)
    lasts = [zc[c_len - 1:c_len, :] for zc in zcs]
    return outs, s_news, lasts


def _rwkv_params(l, p):
    zeros = jnp.zeros((HEAD_DIM, D_C), F32)
    w2p = jnp.concatenate([p['rw_w2'][l], zeros], axis=0).astype(BF16)
    a2p = jnp.concatenate([zeros, p['rw_a2'][l]], axis=0).astype(BF16)
    row = lambda a: a.reshape(1, -1)
    return [row(p['rw_mu'][l]), row(p['rw_w0'][l]), w2p, row(p['rw_a0'][l]), a2p,
            p['rw_g2'][l].astype(BF16), row(p['rw_k_k'][l]), row(p['rw_k_a'][l])]


def _rwkv_prompt(z, bsz, t, l, p):
    n_c = t // RW_CHUNK
    params = _rwkv_params(l, p) + [p['rw_r_k'][l].reshape(1, D_C), p['rw_ln_g'][l].reshape(1, D_C),
                                   p['rw_ln_b'][l].reshape(1, D_C)]
    const = lambda a: pl.BlockSpec(a.shape, lambda b, c: (0,) * a.ndim)
    nb = math.gcd(bsz, RW_SEQS_PER_STEP)
    yc, s_bd = pl.pallas_call(
        _rwkv_kernel,
        grid=(bsz // nb, n_c),
        in_specs=[pl.BlockSpec((nb, RW_CHUNK, C_IN), lambda b, c: (b, c, Z_AB // C_IN))]
                 + [const(a) for a in params],
        out_specs=[pl.BlockSpec((nb, RW_CHUNK, D_C), lambda b, c: (b, c, 0)),
                   pl.BlockSpec((nb, D_C, D_C), lambda b, c: (b, 0, 0))],
        out_shape=[jax.ShapeDtypeStruct((bsz, t, D_C), BF16),
                   jax.ShapeDtypeStruct((bsz, D_C, D_C), F32)],
        scratch_shapes=[pltpu.VMEM((nb, 1, C_IN), F32)],
        compiler_params=_cparams(("parallel", "arbitrary")),
        name="rwkv_chunk",
    )(z.reshape(bsz, t, D_IN), *params)
    yc = yc.reshape(bsz * t, D_C)
    s_heads = jnp.stack([s_bd[:, h * HEAD_DIM:(h + 1) * HEAD_DIM, h * HEAD_DIM:(h + 1) * HEAD_DIM]
                         for h in range(N_HEADS)], axis=1)
    return yc, s_heads


SAMPLE_SLOTS = 8
SAMPLE_WKV_VBLOCK = 32
SAMPLE_RET_SEQS = 16


def _samp_prep_kernel(z_ref, prev_ref, cos_ref, sin_ref, ws0_ref, bs0_ref, gain_ref, mu_ref, w0_ref,
                      w2p_ref, a0_ref, a2p_ref, g2_ref, kk_ref, ka_ref, vec_ref, qk_ref, vt_ref, v_t_ref):
    gu = _gelu(z_ref[:, 0:D_A])
    gv = _gelu(z_ref[:, D_A:2 * D_A])
    mean_sel = _group_sel(D_A, D_A, HEAD_DIM, HEAD_DIM, 1.0 / HEAD_DIM)
    vn = gv * lax.rsqrt(_dot_sel(gv * gv, mean_sel) + NORM_EPS) * gain_ref[...]
    vec_ref[0] = gu * (ws0_ref[...] * vn + bs0_ref[...])
    vec_ref[1] = vn
    q = _rotate_half(z_ref[:, 2 * D_A:3 * D_A], cos_ref[...], sin_ref[...])
    k = _rotate_half(z_ref[:, 3 * D_A:4 * D_A], cos_ref[...], sin_ref[...]) * (HEAD_DIM ** -0.5)
    vec_ref[2] = q
    vec_ref[3] = k
    qk_ref[...] = _dot_sel(q * k, _group_sel(D_A, D_BV, HEAD_DIM, DV_B))
    r, k2, v, logw, a, g, kk = _rwkv_vectors(z_ref[:, Z_AB:D_IN], prev_ref[...], mu_ref[...], w0_ref[...],
                                             w2p_ref[...], a0_ref[...], a2p_ref[...], g2_ref[...],
                                             kk_ref[...], ka_ref[...])
    vec_ref[4] = r
    vec_ref[5] = k2
    vec_ref[6] = v
    vec_ref[7] = g
    for i, x in enumerate((jnp.exp(logw), kk, kk * a, k2, r)):
        vt_ref[i] = x.T
    v_t_ref[...] = v.T


def _samp_wkv_kernel(s_ref, vt_ref, v_ref, so_ref, y_ref):
    w, kk, b, k2, r = [vt_ref[i] for i in range(5)]
    for j in range(s_ref.shape[0]):
        s = s_ref[j]
        sa = jnp.sum(s * (-kk), axis=0, keepdims=True)
        s = s * w + sa * b + v_ref[j:j + 1, :] * k2
        so_ref[j] = s
        y_ref[j:j + 1, :] = jnp.sum(s * r, axis=0, keepdims=True)


def _samp_state_kernel(sret_ref, rows_ref, vr_ref, sdec_ref, sret_o_ref, cross_ref):
    q, k = [rows_ref[:, :, i:i + 1, :] for i in range(2)]
    eye = (lax.broadcasted_iota(jnp.int32, (HEAD_DIM, HEAD_DIM), 0)
           == lax.broadcasted_iota(jnp.int32, (HEAD_DIM, HEAD_DIM), 1))

    def to_col(row):
        return jnp.sum(jnp.where(eye, row, 0.0), axis=3, keepdims=True)

    s = sret_ref[...]
    cross_ref[...] = jnp.sum(s * to_col(q), axis=2, keepdims=True)
    sret_o_ref[...] = s * sdec_ref[...][None] + to_col(k) * vr_ref[...]


def _samp_post_kernel(vec_ref, qk_ref, cross_ref, y_t_ref, zv_ref, zg_ref, qdec_ref, gng_ref, rk_ref,
                      lng_ref, lnb_ref, yab_ref, yc_ref):
    yab_ref[:, 0:D_A] = vec_ref[0].astype(BF16)
    ob = qk_ref[...] * zv_ref[...] + cross_ref[...] * qdec_ref[...]
    for h in range(N_HEADS):
        cols = slice(h * DV_B, (h + 1) * DV_B)
        yb = _rms(ob[:, cols]) * gng_ref[:, cols] * _silu(zg_ref[:, cols])
        yab_ref[:, D_A + h * DV_B:D_A + (h + 1) * DV_B] = yb.astype(BF16)
    out = _rwkv_output(y_t_ref[...].T, vec_ref[4], vec_ref[5], vec_ref[6], vec_ref[7], rk_ref[...],
                       lng_ref[...], lnb_ref[...])
    yc_ref[...] = out.astype(BF16)


def _mix_sample(z, s_ret_all, s_wkv_t_all, shift, l, p):
    bs = z.shape[0]
    bb = math.gcd(bs, SAMPLE_RET_SEQS)
    vb = SAMPLE_WKV_VBLOCK
    full = lambda a: pl.BlockSpec(a.shape, lambda *_: (0,) * a.ndim)
    cos_t, sin_t = _rope_tables(jnp.full((1,), PAST_LEN, F32))
    lg = _retention_log_decay()
    ws0 = jnp.repeat(p['sg_w_s'][l][:, 0, 0], HEAD_DIM).reshape(1, D_A)
    bs0 = jnp.repeat(p['sg_b_s'][l][:, 0], HEAD_DIM).reshape(1, D_A)
    gain = p['sg_v_gain'][l].reshape(1, D_A)
    prep_args = [z, shift, cos_t, sin_t, ws0, bs0, gain] + _rwkv_params(l, p)
    prep_shapes = [(SAMPLE_SLOTS, bs, D_A), (bs, D_BV), (5, D_C, bs), (D_C, bs)]
    vec, qk, vt, v_t = pl.pallas_call(
        _samp_prep_kernel,
        grid=(1,),
        in_specs=[full(a) for a in prep_args],
        out_specs=[pl.BlockSpec(s, lambda i, n=len(s): (0,) * n) for s in prep_shapes],
        out_shape=[jax.ShapeDtypeStruct(s, F32) for s in prep_shapes],
        compiler_params=_cparams(("arbitrary",)),
        name="sample_prep",
    )(*prep_args)

    s_wkv_new_t, y_t = pl.pallas_call(
        _samp_wkv_kernel,
        grid=(N_HEADS, HEAD_DIM // vb),
        in_specs=[pl.BlockSpec((None, None, vb, HEAD_DIM, bs), lambda h, i: (l, h, i, 0, 0)),
                  pl.BlockSpec((5, None, HEAD_DIM, bs), lambda h, i: (0, h, 0, 0)),
                  pl.BlockSpec((None, vb, bs), lambda h, i: (h, i, 0))],
        out_specs=[pl.BlockSpec((None, vb, HEAD_DIM, bs), lambda h, i: (h, i, 0, 0)),
                   pl.BlockSpec((None, vb, bs), lambda h, i: (h, i, 0))],
        out_shape=[jax.ShapeDtypeStruct((N_HEADS, HEAD_DIM, HEAD_DIM, bs), F32),
                   jax.ShapeDtypeStruct((N_HEADS, HEAD_DIM, bs), F32)],
        compiler_params=_cparams(("parallel", "parallel")),
        name="sample_wkv",
    )(s_wkv_t_all, vt.reshape(5, N_HEADS, HEAD_DIM, bs), v_t.reshape(N_HEADS, HEAD_DIM, bs))

    zv = z[:, 4 * D_A:4 * D_A + D_BV]
    zg = z[:, 4 * D_A + D_BV:Z_AB]
    sdec_t = jnp.broadcast_to(jnp.exp(lg)[:, None, None], (N_HEADS, HEAD_DIM, DV_B))
    rows = jnp.transpose(vec[2:4].reshape(2, bs, N_HEADS, HEAD_DIM), (1, 2, 0, 3))
    blk = lambda shape: pl.BlockSpec((bb,) + shape[1:], lambda i: (i, 0, 0, 0))
    out_shapes = [jax.ShapeDtypeStruct((bs, N_HEADS, HEAD_DIM, DV_B), F32),
                  jax.ShapeDtypeStruct((bs, N_HEADS, 1, DV_B), F32)]
    s_ret_new, cross = pl.pallas_call(
        _samp_state_kernel,
        grid=(bs // bb,),
        in_specs=[pl.BlockSpec((None, bb) + s_ret_all.shape[2:], lambda i: (l, i, 0, 0, 0)),
                  blk(rows.shape), blk((bs, N_HEADS, 1, DV_B)),
                  pl.BlockSpec(sdec_t.shape, lambda i: (0, 0, 0))],
        out_specs=[blk(a.shape) for a in out_shapes],
        out_shape=out_shapes,
        compiler_params=_cparams(("parallel",)),
        name="sample_state",
    )(s_ret_all, rows, zv.reshape(bs, N_HEADS, 1, DV_B), sdec_t)

    qdec_t = jnp.repeat(jnp.exp(lg), DV_B).reshape(1, D_BV)
    post_args = [vec, qk, cross.reshape(bs, D_BV), y_t.reshape(D_C, bs), zv, zg, qdec_t,
                 p['ret_gn_g'][l].reshape(1, D_BV), p['rw_r_k'][l].reshape(1, D_C),
                 p['rw_ln_g'][l].reshape(1, D_C), p['rw_ln_b'][l].reshape(1, D_C)]
    yab, yc = pl.pallas_call(
        _samp_post_kernel,
        grid=(1,),
        in_specs=[full(a) for a in post_args],
        out_specs=[pl.BlockSpec((bs, D_AB), lambda i: (0, 0)), pl.BlockSpec((bs, D_C), lambda i: (0, 0))],
        out_shape=[jax.ShapeDtypeStruct((bs, D_AB), BF16), jax.ShapeDtypeStruct((bs, D_C), BF16)],
        compiler_params=_cparams(("arbitrary",)),
        name="sample_post",
    )(*post_args)
    return yab, yc, s_ret_new, s_wkv_new_t, vec[1]


def _row_tile(n_rows_per_seq):
    for t in (512, 256, 128):
        if n_rows_per_seq % t == 0:
            return t
    raise ValueError("sequence length must be a multiple of 128")


def kernel(x_prompt, x_sample, state_ret, state_wkv, state_shift, c_prompt, c_sample,
           w_ada, b_ada, w_ffn1_in, w_ffn1_out, w_in, w_out, w_ffn2_in, w_ffn2_out,
           sg_v_gain, sg_w_s, sg_b_s, ret_gn_g,
           rw_mu, rw_w0, rw_w2, rw_a0, rw_a2, rw_g2, rw_k_k, rw_k_a, rw_r_k, rw_ln_g, rw_ln_b,
           final_g):
    p = dict(sg_v_gain=sg_v_gain, sg_w_s=sg_w_s, sg_b_s=sg_b_s, ret_gn_g=ret_gn_g, rw_mu=rw_mu,
             rw_w0=rw_w0, rw_w2=rw_w2, rw_a0=rw_a0, rw_a2=rw_a2, rw_g2=rw_g2, rw_k_k=rw_k_k,
             rw_k_a=rw_k_a, rw_r_k=rw_r_k, rw_ln_g=rw_ln_g, rw_ln_b=rw_ln_b, final_g=final_g)
    wb = dict(ffn1_in=w_ffn1_in.astype(BF16), ffn1_out=w_ffn1_out.astype(BF16), w_in=w_in.astype(BF16),
              w_out=w_out.astype(BF16), ffn2_in=w_ffn2_in.astype(BF16), ffn2_out=w_ffn2_out.astype(BF16))
    bp, t, _ = x_prompt.shape
    bs = x_sample.shape[0]
    assert x_sample.shape[1] == 1 and t % CHUNK == 0

    bp_pad = -(-bp // SUBLANES) * SUBLANES
    assert bs % bp_pad == 0
    c_all = jnp.concatenate([c_sample, c_prompt, jnp.zeros((bp_pad - bp, D_MODEL), F32)], axis=0)
    mod = _ada(c_all, w_ada, b_ada)

    def prompt_mixer(z, l):
        yab, s_ret = _mix_ab_prompt(z, bp, t, l, p)
        yc, s_wkv = _rwkv_prompt(z, bp, t, l, p)
        last = z.reshape(bp, t, D_IN)[:, t - 1, Z_AB:]
        return yab, yc, (s_ret, s_wkv, last)

    tile_p = _row_tile(t)
    def run(x, make_rows, mixer):
        extras = []
        for l in range(DEPTH):
            rows = make_rows(l)
            x = _ffn(x, rows, l, 0, wb['ffn1_in'], wb['ffn1_out'])
            z = _proj_in(x, rows, l, wb['w_in'])
            yab, yc, extra = mixer(z, l)
            x = _ffn(x, rows, l, 6, wb['ffn2_in'], wb['ffn2_out'],
                     final_g=final_g if l == DEPTH - 1 else None, mixer=(yab, yc, wb['w_out']))
            extras.append(extra)
        return x, extras

    rows_p = lambda l: _Rows(mod, l, bp * t, tile_p, tiles_per_seq=t // tile_p, n_seq=bp_pad,
                             seq_block=bs // bp_pad)
    y_p, ex_p = run(x_prompt.reshape(bp * t, D_MODEL), rows_p, prompt_mixer)

    state_wkv_t = jnp.transpose(state_wkv, (0, 2, 3, 4, 1))

    def sample_mixer(z, l):
        yab, yc, s_ret, s_wkv_t, vn = _mix_sample(z, state_ret, state_wkv_t, state_shift[l], l, p)
        return yab, yc, (s_ret, s_wkv_t, z[:, Z_AB:], vn)

    rows_s = lambda l: _Rows(mod, l, bs, bs)
    y_s, ex_s = run(x_sample.reshape(bs, D_MODEL), rows_s, sample_mixer)

    stack = lambda ex, i: jnp.stack([e[i] for e in ex])
    return (y_p.reshape(bp, t, D_MODEL), y_s.reshape(bs, 1, D_MODEL),
            stack(ex_p, 0), stack(ex_p, 1), stack(ex_p, 2),
            stack(ex_s, 0), jnp.transpose(stack(ex_s, 1), (0, 4, 1, 2, 3)), stack(ex_s, 2),
            stack(ex_s, 3).reshape(DEPTH, bs, 1, D_A))
```

```python
import functools
import math

import jax
import jax.numpy as jnp
from jax import lax
from jax.experimental import pallas as pl
from jax.experimental.pallas import tpu as pltpu

F32 = jnp.float32
BF16 = jnp.bfloat16

D_MODEL = 1024
DEPTH = 4
N_MOD = 9
D_FF = 2816
N_HEADS = 4
HEAD_DIM = 64
DV_B = 128
CHUNK = 128
RW_CHUNK = 64
AB_SEQS_PER_STEP = 8
RW_SEQS_PER_STEP = 8
D_A = N_HEADS * HEAD_DIM
D_BV = N_HEADS * DV_B
D_C = N_HEADS * HEAD_DIM
C_IN = 1024
D_IN = 3072
Z_AB = 2048
D_AB = D_A + D_BV
ROPE_BASE = 10000.0
NORM_EPS = 1e-6
GN_EPS = 64e-5
PAST_LEN = 16384

VMEM_LIMIT_BYTES = 56 * 1024 * 1024
SUBLANES = 8
MXU_WIDTH = 256
FFN_CHUNKS = ((0, 6 * MXU_WIDTH), (6 * MXU_WIDTH, 5 * MXU_WIDTH))


def _cparams(sem):
    return pltpu.CompilerParams(dimension_semantics=sem, vmem_limit_bytes=VMEM_LIMIT_BYTES)


def _dot(a, b):
    return jnp.dot(a.astype(BF16), b.astype(BF16), preferred_element_type=F32)


def _dot_t(a, b):
    return lax.dot_general(a.astype(BF16), b.astype(BF16), (((1,), (1,)), ((), ())),
                           preferred_element_type=F32)


def _split2(x):
    hi = x.astype(BF16)
    lo = (x - hi.astype(F32)).astype(BF16)
    return hi, lo


def _split3(x):
    hi = x.astype(BF16)
    r1 = x - hi.astype(F32)
    mid = r1.astype(BF16)
    lo = (r1 - mid.astype(F32)).astype(BF16)
    return hi, mid, lo


def _dot_sel(x, sel):
    m = x.shape[0]
    out = jnp.dot(jnp.concatenate(_split2(x), axis=0), sel, preferred_element_type=F32)
    return out[0:m] + out[m:2 * m]


def _dot3_general(a, b, dn):
    ah, al = a if isinstance(a, tuple) else _split2(a)
    bh, bl = b if isinstance(b, tuple) else _split2(b)
    m = ah.shape[0]
    top = lax.dot_general(jnp.concatenate([ah, al], axis=0), bh, dn, preferred_element_type=F32)
    return top[0:m] + top[m:2 * m] + lax.dot_general(ah, bl, dn, preferred_element_type=F32)


_DN_AB = (((1,), (0,)), ((), ()))
_DN_ABT = (((1,), (1,)), ((), ()))


def _dot3_with_rider(a, rider, b, dn):
    ah, al = a if isinstance(a, tuple) else _split2(a)
    bh, bl = b if isinstance(b, tuple) else _split2(b)
    m, n = ah.shape[0], rider.shape[0]
    top = lax.dot_general(jnp.concatenate([ah, al, rider.astype(BF16)], axis=0), bh, dn,
                          preferred_element_type=F32)
    main = top[0:m] + top[m:2 * m] + lax.dot_general(ah, bl, dn, preferred_element_type=F32)
    return main, top[2 * m:2 * m + n]


def _dot3(a, b):
    return _dot3_general(a, b, _DN_AB)


def _rms(x, eps=NORM_EPS):
    return x * lax.rsqrt(jnp.mean(x * x, axis=-1, keepdims=True) + eps)


def _silu(x):
    return x * jax.nn.sigmoid(x)


def _gelu(x):
    c = math.sqrt(2.0 / math.pi)
    return (0.5 * x) * (1.0 + jnp.tanh(x * (c + (c * 0.044715) * (x * x))))


def _head_of(shape, dim, width):
    return lax.broadcasted_iota(jnp.int32, shape, dim) // width


def _group_sel(n_in, n_out, w_in, w_out, scale=1.0):
    gi = _head_of((n_in, n_out), 0, w_in)
    go = _head_of((n_in, n_out), 1, w_out)
    return jnp.where(gi == go, scale, 0.0).astype(BF16)


def _stack_heads(x, width):
    lane_head = _head_of(x.shape, 1, width)
    return jnp.concatenate([jnp.where(lane_head == h, x, 0.0) for h in range(N_HEADS)], axis=0)


def _rotate_half(x, cos_t, sin_t):
    n = x.shape[1]
    half = HEAD_DIM // 2
    lane = lax.broadcasted_iota(jnp.int32, x.shape, 1)
    fwd = pltpu.roll(x, n - half, axis=1)
    bwd = pltpu.roll(x, half, axis=1)
    partner = jnp.where((lane % HEAD_DIM) < half, fwd, bwd)
    return x * cos_t + partner * sin_t


ADA_VECS_PER_STEP = 3


def _ada_kernel(c_ref, w_ref, b_ref, o_ref):
    cs = _silu(c_ref[...]).astype(BF16)
    for j in range(ADA_VECS_PER_STEP):
        cols = slice(j * D_MODEL, (j + 1) * D_MODEL)
        o_ref[j] = _dot(cs, w_ref[:, cols]) + b_ref[:, cols]


def _ada(c_all, w_ada, b_ada):
    bt = c_all.shape[0]
    nv = ADA_VECS_PER_STEP
    return pl.pallas_call(
        _ada_kernel,
        grid=(DEPTH, N_MOD // nv),
        in_specs=[pl.BlockSpec((bt, D_MODEL), lambda l, j: (0, 0)),
                  pl.BlockSpec((None, D_MODEL, nv * D_MODEL), lambda l, j: (l, 0, j)),
                  pl.BlockSpec((None, 1, nv * D_MODEL), lambda l, j: (l, 0, j))],
        out_specs=pl.BlockSpec((None, nv, bt, D_MODEL), lambda l, j: (l, j, 0, 0)),
        out_shape=jax.ShapeDtypeStruct((DEPTH, N_MOD, bt, D_MODEL), F32),
        compiler_params=_cparams(("parallel", "parallel")),
        name="ada_mod",
    )(c_all, w_ada, b_ada.reshape(DEPTH, 1, N_MOD * D_MODEL))


class _Rows:
    def __init__(self, mod, layer, n_rows, tile, tiles_per_seq=None, n_seq=None, seq_block=None, start=0):
        self.mod, self.layer, self.n_rows, self.tile = mod, layer, n_rows, tile
        self.n_tiles = n_rows // tile
        self.tiles_per_seq, self.n_seq, self.seq_block = tiles_per_seq, n_seq, seq_block
        self.start = start
        assert tiles_per_seq is None or start == 0

    def _tile_index(self, i):
        return jnp.clip(i - self.start, 0, self.n_tiles - 1)

    def mod_spec(self, j):
        l = self.layer
        if self.tiles_per_seq is None:
            return pl.BlockSpec((None, None, self.tile, D_MODEL), lambda i: (l, j, self._tile_index(i), 0))
        blk = self.seq_block
        return pl.BlockSpec((None, None, self.n_seq, D_MODEL), lambda i: (l, j, blk, 0))

    def row_spec(self, width):
        return pl.BlockSpec((self.tile, width), lambda i: (self._tile_index(i), 0))


def _mod_row(ref, tiles_per_seq):
    if tiles_per_seq is None:
        return ref[...]
    return ref[pl.ds(pl.program_id(0) // tiles_per_seq, 1), :]


def _in_group(rows_groups, gi):
    if len(rows_groups) == 1:
        return None
    i = pl.program_id(0)
    start, n_tiles = rows_groups[gi][1], rows_groups[gi][2]
    return jnp.logical_and(i >= start, i < start + n_tiles)


def _ffn_kernel(*refs, final, mixer_out, groups):
    refs = list(refs)
    o_refs = [refs.pop() for _ in groups][::-1]
    wi_ref, wo_ref = refs[:2]
    refs = refs[2:]
    mix_w = None
    if mixer_out:
        mix_w, refs = refs[:2], refs[2:]
    fg_ref = None
    if final:
        fg_ref, refs = refs[0], refs[1:]
    per_group = len(refs) // len(groups)
    for gi, group in enumerate(groups):
        g_refs = refs[gi * per_group:(gi + 1) * per_group]
        body = functools.partial(_ffn_rows, g_refs, wi_ref, wo_ref, mix_w, fg_ref, o_refs[gi], group[0])
        pred = _in_group(groups, gi)
        if pred is None:
            body()
        else:
            pl.when(pred)(body)


def _ffn_rows(g_refs, wi_ref, wo_ref, mix_w, fg_ref, o_ref, tiles_per_seq):
    x_ref, sh_ref, sc_ref, g_ref = g_refs[:4]
    mod = functools.partial(_mod_row, tiles_per_seq=tiles_per_seq)
    x = x_ref[...]
    if mix_w is not None:
        gm_ref, yab_ref, yc_ref = g_refs[4:7]
        wab_ref, wc_ref = mix_w
        x = x + mod(gm_ref) * (jnp.dot(yab_ref[...], wab_ref[...], preferred_element_type=F32)
                               + jnp.dot(yc_ref[...], wc_ref[...], preferred_element_type=F32))
    final = fg_ref is not None
    hb = (_rms(x) * (1.0 + mod(sc_ref)) + mod(sh_ref)).astype(BF16)
    y = None
    for c0, cw in FFN_CHUNKS:
        gate = jnp.dot(hb, wi_ref[:, c0:c0 + cw], preferred_element_type=F32)
        up = jnp.dot(hb, wi_ref[:, D_FF + c0:D_FF + c0 + cw], preferred_element_type=F32)
        part = jnp.dot((_silu(gate) * up).astype(BF16), wo_ref[c0:c0 + cw, :], preferred_element_type=F32)
        y = part if y is None else y + part
    out = x + 0.5 * mod(g_ref) * y
    if final:
        out = _rms(out) * fg_ref[...]
    o_ref[...] = out


def _resident(shape, index_map):
    return pl.BlockSpec(shape, index_map, pipeline_mode=pl.Buffered(1))


def _group_meta(rows_list):
    return tuple((r.tiles_per_seq, r.start, r.n_tiles) for r in rows_list)


def _ffn(xs, rows_list, l, jm, w_in_b, w_out_b, final_g=None, mixer=None):
    final = final_g is not None
    in_specs = [_resident((None, D_MODEL, 2 * D_FF), lambda i: (l, 0, 0)),
                _resident((None, D_FF, D_MODEL), lambda i: (l, 0, 0))]
    args = [w_in_b, w_out_b]
    if mixer is not None:
        in_specs += [_resident((None, D_AB, D_MODEL), lambda i: (l, 0, 0)),
                     _resident((None, D_C, D_MODEL), lambda i: (l, D_AB // D_C, 0))]
        args += [mixer[2], mixer[2]]
    if final:
        in_specs.append(pl.BlockSpec((1, D_MODEL), lambda i: (0, 0)))
        args.append(final_g.reshape(1, D_MODEL))
    for gi, (x, rows) in enumerate(zip(xs, rows_list)):
        in_specs += [rows.row_spec(D_MODEL), rows.mod_spec(jm), rows.mod_spec(jm + 1), rows.mod_spec(jm + 2)]
        args += [x, rows.mod, rows.mod, rows.mod]
        if mixer is not None:
            in_specs += [rows.mod_spec(5), rows.row_spec(D_AB), rows.row_spec(D_C)]
            args += [rows.mod, mixer[0][gi], mixer[1][gi]]
    return pl.pallas_call(
        functools.partial(_ffn_kernel, final=final, mixer_out=mixer is not None,
                          groups=_group_meta(rows_list)),
        grid=(sum(r.n_tiles for r in rows_list),),
        in_specs=in_specs,
        out_specs=[r.row_spec(D_MODEL) for r in rows_list],
        out_shape=[jax.ShapeDtypeStruct((r.n_rows, D_MODEL), F32) for r in rows_list],
        compiler_params=_cparams(("arbitrary",)),
        name="ffn_final" if final else "ffn",
    )(*args)


def _proj_in_kernel(*refs, groups):
    refs = list(refs)
    z_refs = [refs.pop() for _ in groups][::-1]
    w_ref, refs = refs[0], refs[1:]
    for gi, group in enumerate(groups):
        body = functools.partial(_proj_in_rows, *refs[3 * gi:3 * gi + 3], w_ref, z_refs[gi], group[0])
        pred = _in_group(groups, gi)
        if pred is None:
            body()
        else:
            pl.when(pred)(body)


def _proj_in_rows(x_ref, sh_ref, sc_ref, w_ref, z_ref, tiles_per_seq):
    sh, sc = _mod_row(sh_ref, tiles_per_seq), _mod_row(sc_ref, tiles_per_seq)
    h = (_rms(x_ref[...]) * (1.0 + sc) + sh).astype(BF16)
    z_ref[...] = jnp.dot(h, w_ref[...], preferred_element_type=F32)


def _proj_in(xs, rows_list, l, w_in_b):
    in_specs = [_resident((None, D_MODEL, D_IN), lambda i: (l, 0, 0))]
    args = [w_in_b]
    for x, rows in zip(xs, rows_list):
        in_specs += [rows.row_spec(D_MODEL), rows.mod_spec(3), rows.mod_spec(4)]
        args += [x, rows.mod, rows.mod]
    return pl.pallas_call(
        functools.partial(_proj_in_kernel, groups=_group_meta(rows_list)),
        grid=(sum(r.n_tiles for r in rows_list),),
        in_specs=in_specs,
        out_specs=[r.row_spec(D_IN) for r in rows_list],
        out_shape=[jax.ShapeDtypeStruct((r.n_rows, D_IN), F32) for r in rows_list],
        compiler_params=_cparams(("arbitrary",)),
        name="proj_in",
    )(*args)


def _retention_log_decay():
    return jnp.log1p(-jnp.exp(jnp.linspace(math.log(1.0 / 32.0), math.log(1.0 / 512.0), N_HEADS, dtype=F32)))


def _rope_tables(pos):
    half = HEAD_DIM // 2
    inv = ROPE_BASE ** (-jnp.arange(half, dtype=F32) / half)
    ang = pos[:, None] * inv[None, :]
    cos, sin = jnp.cos(ang), jnp.sin(ang)
    cos_t = jnp.tile(jnp.concatenate([cos, cos], axis=-1), (1, N_HEADS))
    sin_t = jnp.tile(jnp.concatenate([-sin, sin], axis=-1), (1, N_HEADS))
    return cos_t, sin_t


def _retention_chunk_tables(c):
    lg = _retention_log_decay()
    idx = jnp.arange(c, dtype=F32)
    diff = idx[:, None] - idx[None, :]
    dmask = jnp.where(diff[None] >= 0, jnp.exp(jnp.maximum(diff, 0.0)[None] * lg[:, None, None]), 0.0)
    q_dec = jnp.exp((idx[:, None] + 1.0) * lg[None, :])
    k_dec = jnp.exp((c - 1.0 - idx)[:, None] * lg[None, :])
    s_dec = jnp.exp(c * lg)
    qdec_t = jnp.repeat(q_dec, DV_B, axis=1)
    kdec_t = jnp.repeat(k_dec, HEAD_DIM, axis=1)
    sdec_t = jnp.broadcast_to(jnp.repeat(s_dec, HEAD_DIM)[:, None], (N_HEADS * HEAD_DIM, DV_B))
    return dmask, qdec_t, kdec_t, sdec_t


def _mix_ab_kernel(z_ref, cos_ref, sin_ref, ws_ref, bs_ref, gain_ref, dmask_ref, qdec_ref, kdec_ref,
                   sdec_ref, gng_ref, yab_ref, s_ref):
    @pl.when(pl.program_id(1) == 0)
    def _():
        s_ref[...] = jnp.zeros_like(s_ref)

    seqs = range(z_ref.shape[0])
    mean_sel = _group_sel(D_A, D_A, HEAD_DIM, HEAD_DIM, 1.0 / HEAD_DIM)
    gv = [_gelu(z_ref[i, :, D_A:2 * D_A]) for i in seqs]
    vn = [g * lax.rsqrt(_dot_sel(g * g, mean_sel) + NORM_EPS) * gain_ref[...] for g in gv]
    lane_head = _head_of((CHUNK, D_A), 1, HEAD_DIM)
    row = lax.broadcasted_iota(jnp.int32, (CHUNK, CHUNK), 0)
    col = lax.broadcasted_iota(jnp.int32, (CHUNK, CHUNK), 1)
    mixed = [bs_ref[...] for _ in seqs]
    for h in range(N_HEADS):
        w_m = jnp.where(row >= col, ws_ref[h], 0.0).astype(BF16)
        mixed = [m + _dot(w_m, jnp.where(lane_head == h, v, 0.0)) for m, v in zip(mixed, vn)]
    for i in seqs:
        yab_ref[i, :, 0:D_A] = (_gelu(z_ref[i, :, 0:D_A]) * mixed[i]).astype(BF16)

    q = [_rotate_half(z_ref[i, :, 2 * D_A:3 * D_A], cos_ref[...], sin_ref[...]).astype(BF16) for i in seqs]
    k = [_rotate_half(z_ref[i, :, 3 * D_A:4 * D_A], cos_ref[...], sin_ref[...]) * (HEAD_DIM ** -0.5)
         for i in seqs]
    s_all = [s_ref[i] for i in seqs]
    kd_t = [(k_i * kdec_ref[...]).T for k_i in k]
    kb = [k_i.astype(BF16) for k_i in k]
    sb = [s.astype(BF16) for s in s_all]
    for h in range(N_HEADS):
        cols_v = slice(4 * D_A + h * DV_B, 4 * D_A + (h + 1) * DV_B)
        cols_g = slice(4 * D_A + D_BV + h * DV_B, 4 * D_A + D_BV + (h + 1) * DV_B)
        cols_h = slice(h * DV_B, (h + 1) * DV_B)
        rows_h = slice(h * HEAD_DIM, (h + 1) * HEAD_DIM)
        v_h = [z_ref[i, :, cols_v] for i in seqs]
        q_h = [jnp.where(lane_head == h, q_i, jnp.zeros_like(q_i)) for q_i in q]
        scores = [_dot_t(q_i, k_i) * dmask_ref[h] for q_i, k_i in zip(q_h, kb)]
        inner = [_dot(sc, v_i) for sc, v_i in zip(scores, v_h)]
        cross = [jnp.dot(q_i, s_i, preferred_element_type=F32) * qdec_ref[:, cols_h]
                 for q_i, s_i in zip(q_h, sb)]
        upd = [_dot(kd_i[rows_h, :], v_i) for kd_i, v_i in zip(kd_t, v_h)]
        for i in seqs:
            yb = _rms(inner[i] + cross[i]) * gng_ref[:, cols_h] * _silu(z_ref[i, :, cols_g])
            yab_ref[i, :, D_A + h * DV_B:D_A + (h + 1) * DV_B] = yb.astype(BF16)
            s_ref[i, rows_h, :] = s_all[i][rows_h, :] * sdec_ref[rows_h, :] + upd[i]


def _mix_ab_prompt(z, bsz, t, l, p):
    n_c = t // CHUNK
    pos = jnp.arange(t, dtype=F32)
    cos_t, sin_t = _rope_tables(pos)
    dmask, qdec_t, kdec_t, sdec_t = _retention_chunk_tables(CHUNK)
    bs_t = jnp.repeat(p['sg_b_s'][l].T, HEAD_DIM, axis=1)
    gain = p['sg_v_gain'][l].reshape(1, D_A)
    gng = p['ret_gn_g'][l].reshape(1, D_BV)
    const = lambda shape: pl.BlockSpec(shape, lambda b, c: (0,) * len(shape))
    nb = math.gcd(bsz, AB_SEQS_PER_STEP)
    yab, s_ret = pl.pallas_call(
        _mix_ab_kernel,
        grid=(bsz // nb, n_c),
        in_specs=[pl.BlockSpec((nb, CHUNK, Z_AB), lambda b, c: (b, c, 0)),
                  pl.BlockSpec((CHUNK, D_A), lambda b, c: (c, 0)),
                  pl.BlockSpec((CHUNK, D_A), lambda b, c: (c, 0)),
                  pl.BlockSpec((None, N_HEADS, CHUNK, CHUNK), lambda b, c: (l, 0, 0, 0)),
                  const((CHUNK, D_A)), const((1, D_A)), const((N_HEADS, CHUNK, CHUNK)),
                  const((CHUNK, D_BV)), const((CHUNK, D_A)), const((D_A, DV_B)), const((1, D_BV))],
        out_specs=[pl.BlockSpec((nb, CHUNK, D_AB), lambda b, c: (b, c, 0)),
                   pl.BlockSpec((nb, D_A, DV_B), lambda b, c: (b, 0, 0))],
        out_shape=[jax.ShapeDtypeStruct((bsz, t, D_AB), BF16),
                   jax.ShapeDtypeStruct((bsz, D_A, DV_B), F32)],
        compiler_params=_cparams(("parallel", "arbitrary")),
        name="mix_ab",
    )(z.reshape(bsz, t, D_IN), cos_t, sin_t, p['sg_w_s'], bs_t, gain, dmask, qdec_t, kdec_t, sdec_t, gng)
    return yab.reshape(bsz * t, D_AB), s_ret.reshape(bsz, N_HEADS, HEAD_DIM, DV_B)


def _softplus(x):
    return jnp.maximum(x, 0.0) + jnp.log(1.0 + jnp.exp(-jnp.abs(x)))


def _rwkv_vectors(zc, shifted, mu, w0, w2p, a0, a2p, g2, k_k, k_a):
    zs = zc + (shifted - zc) * mu
    r = zs[:, 0:D_C]
    k = zs[:, D_C:2 * D_C]
    v = zs[:, 2 * D_C:3 * D_C]
    xwa = zs[:, 3 * D_C:3 * D_C + 128]
    xg = zs[:, 3 * D_C + 128:C_IN]
    w_log = -_softplus(-(w0 + _dot(jnp.tanh(xwa), w2p))) - 0.5
    logw = -jnp.exp(w_log)
    a = jax.nn.sigmoid(a0 + _dot(xwa, a2p))
    g = _dot(jax.nn.sigmoid(xg), g2)
    kk = k * k_k
    sum_sel = _group_sel(D_C, D_C, HEAD_DIM, HEAD_DIM)
    kk = kk * lax.rsqrt(jnp.maximum(_dot_sel(kk * kk, sum_sel), 1e-24))
    k2 = k * (1.0 + (a - 1.0) * k_a)
    return r, k2, v, logw, a, g, kk


def _rwkv_output(y, r, k2, v, g, r_k, ln_g, ln_b):
    mean_sel = _group_sel(D_C, D_C, HEAD_DIM, HEAD_DIM, 1.0 / HEAD_DIM)
    sum_sel = _group_sel(D_C, D_C, HEAD_DIM, HEAD_DIM)
    yc = y - _dot_sel(y, mean_sel)
    var = _dot_sel(yc * yc, mean_sel)
    yn = yc * lax.rsqrt(var + GN_EPS) * ln_g + ln_b
    bonus = _dot_sel(r * k2 * r_k, sum_sel) * v
    return (yn + bonus) * g


def _rwkv_kernel(zc_ref, mu_ref, w0_ref, w2p_ref, a0_ref, a2p_ref, g2_ref, kk_ref, ka_ref, rk_ref,
                 lng_ref, lnb_ref, yc_ref, s_ref, prev_ref):
    @pl.when(pl.program_id(1) == 0)
    def _():
        s_ref[...] = jnp.zeros_like(s_ref)
        prev_ref[...] = jnp.zeros_like(prev_ref)

    params = [ref[...] for ref in (mu_ref, w0_ref, w2p_ref, a0_ref, a2p_ref, g2_ref, kk_ref, ka_ref,
                                   rk_ref, lng_ref, lnb_ref)]
    n_seq = zc_ref.shape[0]
    outs, s_news, lasts = _rwkv_chunks([zc_ref[i] for i in range(n_seq)],
                                       [prev_ref[i] for i in range(n_seq)],
                                       [s_ref[i] for i in range(n_seq)], params)
    for i in range(n_seq):
        yc_ref[i] = outs[i]
        s_ref[i] = s_news[i]
        prev_ref[i] = lasts[i]


def _each(fn, *lists):
    return [fn(*args) for args in zip(*lists)]


def _rwkv_chunks(zcs, prevs, s_bds, params):
    mu, w0, w2p, a0, a2p, g2, k_k, k_a, r_k, ln_g, ln_b = params
    c_len = RW_CHUNK
    nw = N_HEADS * c_len
    row_id = lax.broadcasted_iota(jnp.int32, zcs[0].shape, 0)
    tri = jnp.where(lax.broadcasted_iota(jnp.int32, (c_len, c_len), 0)
                    >= lax.broadcasted_iota(jnp.int32, (c_len, c_len), 1), 1.0, 0.0).astype(BF16)
    t_id = lax.broadcasted_iota(jnp.int32, (c_len, nw), 0)
    s_id = lax.broadcasted_iota(jnp.int32, (c_len, nw), 1) % c_len
    strict = s_id < t_id
    incl = s_id <= t_id
    bd = _head_of((nw, nw), 0, c_len) == _head_of((nw, nw), 1, c_len)

    n_seq = len(zcs)
    per_seq = lambda x: [x[i * c_len:(i + 1) * c_len] for i in range(n_seq)]
    shifted = _each(lambda zc, prev: jnp.where(row_id == 0, prev, pltpu.roll(zc, 1, axis=0)), zcs, prevs)
    vec_all = _rwkv_vectors(jnp.concatenate(zcs, axis=0), jnp.concatenate(shifted, axis=0),
                            mu, w0, w2p, a0, a2p, g2, k_k, k_a)
    r_all, k2_all, v_all, _, _, g_all, _ = vec_all
    r, k2, v, logw, a, g, kk = [per_seq(x) for x in vec_all]
    b = _each(lambda kk_i, a_i: kk_i * a_i, kk, a)

    def cumsum(lw):
        hi, mid, lo = _split3(lw)
        return (jnp.dot(tri, hi, preferred_element_type=F32) + jnp.dot(tri, mid, preferred_element_type=F32)
                + jnp.dot(tri, lo, preferred_element_type=F32))
    cum = _each(cumsum, logw)
    c_last = [c[c_len - 1:c_len, :] for c in cum]
    a_hat = _each(lambda kk_i, c, lw: -kk_i * jnp.exp(c - lw), kk, cum, logw)
    r_hat = _each(lambda r_i, c: r_i * jnp.exp(c), r, cum)
    def stacked_parts(x):
        return tuple(_stack_heads(part, HEAD_DIM) for part in _split2(x))

    def bk_parts(b_i, k_i, c):
        e_inv = jnp.exp(-c)
        return tuple(jnp.concatenate([bp, kp], axis=0)
                     for bp, kp in zip(stacked_parts(b_i * e_inv), stacked_parts(k_i * e_inv)))
    bk_t = _each(bk_parts, b, k2, cum)
    a_parts = [_split2(ah) for ah in a_hat]
    gram = _each(lambda ap, rh, bk: _dot3_with_rider(ap, rh, bk, _DN_ABT), a_parts, r_hat, bk_t)
    x_w = [jnp.where(strict, ga[:, 0:nw], 0.0) for ga, _ in gram]
    l_ak = [jnp.where(strict, ga[:, nw:2 * nw], 0.0) for ga, _ in gram]
    m_rb = [jnp.where(incl, gr[:, 0:nw], 0.0) for _, gr in gram]
    m_rk = [jnp.where(incl, gr[:, nw:2 * nw], 0.0) for _, gr in gram]

    v_bd = [stacked_parts(v_i) for v_i in v]
    ar_s = _each(lambda ap, rh, s: _dot3_with_rider(ap, rh, s, _DN_ABT), a_parts, r_hat, s_bds)
    lm_v = _each(lambda l, m, vb: _dot3_with_rider(l, m, vb, _DN_AB), l_ak, m_rk, v_bd)
    w_rhs = _each(lambda s_part, v_part: s_part[0] + v_part[0], ar_s, lm_v)

    def expand(m_w):
        return jnp.where(bd, jnp.concatenate([m_w] * N_HEADS, axis=0), jnp.zeros((), m_w.dtype))
    power = [x.astype(BF16) for x in x_w]
    t_inv = [jnp.where(s_id == t_id, 1.0, x) for x in x_w]
    for _ in range(5):
        power = [jnp.dot(pw, expand(pw), preferred_element_type=F32).astype(BF16) for pw in power]
        t_inv = _each(lambda t, pw: t + jnp.dot(pw, expand(t.astype(BF16)), preferred_element_type=F32),
                      t_inv, power)
    t_inv = [t.astype(BF16) for t in t_inv]
    u = _each(lambda t, w: jnp.dot(t, _stack_heads(w.astype(BF16), HEAD_DIM), preferred_element_type=F32),
              t_inv, w_rhs)
    resid = _each(lambda w, u_i, x: w - u_i + _dot3(x, stacked_parts(u_i)), w_rhs, u, x_w)
    u = _each(lambda u_i, t, rs: u_i + jnp.dot(t, _stack_heads(rs.astype(BF16), HEAD_DIM),
                                               preferred_element_type=F32), u, t_inv, resid)
    u_bd = [stacked_parts(u_i) for u_i in u]

    y = _each(lambda s_part, v_part, mb, ub: s_part[1] + v_part[1] + jnp.dot(mb.astype(BF16), ub[0],
                                                                             preferred_element_type=F32),
              ar_s, lm_v, m_rb, u_bd)
    outs = per_seq(_rwkv_output(jnp.concatenate(y, axis=0), r_all, k2_all, v_all, g_all,
                                r_k, ln_g, ln_b).astype(BF16))

    bd_s = _head_of(s_bds[0].shape, 0, HEAD_DIM) == _head_of(s_bds[0].shape, 1, HEAD_DIM)

    def new_state(s, u_i, v_i, b_i, k_i, c, cl):
        e_tail = jnp.exp(cl - c)
        uv_t = jnp.concatenate([u_i, v_i], axis=0).T
        s_upd = _dot3(uv_t, jnp.concatenate([b_i * e_tail, k_i * e_tail], axis=0))
        return s * jnp.exp(cl) + jnp.where(bd_s, s_upd, 0.0)
    s_news = _each(new_state, s_bds, u, v, b, k2, cum, c_last)
    lasts = [zc[c_len - 1:c_len, :] for zc in zcs]
    return outs, s_news, lasts


def _rwkv_params(l, p):
    zeros = jnp.zeros((HEAD_DIM, D_C), F32)
    w2p = jnp.concatenate([p['rw_w2'][l], zeros], axis=0).astype(BF16)
    a2p = jnp.concatenate([zeros, p['rw_a2'][l]], axis=0).astype(BF16)
    row = lambda a: a.reshape(1, -1)
    return [row(p['rw_mu'][l]), row(p['rw_w0'][l]), w2p, row(p['rw_a0'][l]), a2p,
            p['rw_g2'][l].astype(BF16), row(p['rw_k_k'][l]), row(p['rw_k_a'][l])]


def _rwkv_prompt(z, bsz, t, l, p):
    n_c = t // RW_CHUNK
    params = _rwkv_params(l, p) + [p['rw_r_k'][l].reshape(1, D_C), p['rw_ln_g'][l].reshape(1, D_C),
                                   p['rw_ln_b'][l].reshape(1, D_C)]
    const = lambda a: pl.BlockSpec(a.shape, lambda b, c: (0,) * a.ndim)
    nb = math.gcd(bsz, RW_SEQS_PER_STEP)
    yc, s_bd = pl.pallas_call(
        _rwkv_kernel,
        grid=(bsz // nb, n_c),
        in_specs=[pl.BlockSpec((nb, RW_CHUNK, C_IN), lambda b, c: (b, c, Z_AB // C_IN))]
                 + [const(a) for a in params],
        out_specs=[pl.BlockSpec((nb, RW_CHUNK, D_C), lambda b, c: (b, c, 0)),
                   pl.BlockSpec((nb, D_C, D_C), lambda b, c: (b, 0, 0))],
        out_shape=[jax.ShapeDtypeStruct((bsz, t, D_C), BF16),
                   jax.ShapeDtypeStruct((bsz, D_C, D_C), F32)],
        scratch_shapes=[pltpu.VMEM((nb, 1, C_IN), F32)],
        compiler_params=_cparams(("parallel", "arbitrary")),
        name="rwkv_chunk",
    )(z.reshape(bsz, t, D_IN), *params)
    yc = yc.reshape(bsz * t, D_C)
    s_heads = jnp.stack([s_bd[:, h * HEAD_DIM:(h + 1) * HEAD_DIM, h * HEAD_DIM:(h + 1) * HEAD_DIM]
                         for h in range(N_HEADS)], axis=1)
    return yc, s_heads


SAMPLE_SLOTS = 8
SAMPLE_WKV_VBLOCK = 32
SAMPLE_RET_SEQS = 16


def _samp_prep_kernel(z_ref, prev_ref, cos_ref, sin_ref, ws0_ref, bs0_ref, gain_ref, mu_ref, w0_ref,
                      w2p_ref, a0_ref, a2p_ref, g2_ref, kk_ref, ka_ref, vec_ref, qk_ref, vt_ref, v_t_ref):
    gu = _gelu(z_ref[:, 0:D_A])
    gv = _gelu(z_ref[:, D_A:2 * D_A])
    mean_sel = _group_sel(D_A, D_A, HEAD_DIM, HEAD_DIM, 1.0 / HEAD_DIM)
    vn = gv * lax.rsqrt(_dot_sel(gv * gv, mean_sel) + NORM_EPS) * gain_ref[...]
    vec_ref[0] = gu * (ws0_ref[...] * vn + bs0_ref[...])
    vec_ref[1] = vn
    q = _rotate_half(z_ref[:, 2 * D_A:3 * D_A], cos_ref[...], sin_ref[...])
    k = _rotate_half(z_ref[:, 3 * D_A:4 * D_A], cos_ref[...], sin_ref[...]) * (HEAD_DIM ** -0.5)
    vec_ref[2] = q
    vec_ref[3] = k
    qk_ref[...] = _dot_sel(q * k, _group_sel(D_A, D_BV, HEAD_DIM, DV_B))
    r, k2, v, logw, a, g, kk = _rwkv_vectors(z_ref[:, Z_AB:D_IN], prev_ref[...], mu_ref[...], w0_ref[...],
                                             w2p_ref[...], a0_ref[...], a2p_ref[...], g2_ref[...],
                                             kk_ref[...], ka_ref[...])
    vec_ref[4] = r
    vec_ref[5] = k2
    vec_ref[6] = v
    vec_ref[7] = g
    for i, x in enumerate((jnp.exp(logw), kk, kk * a, k2, r)):
        vt_ref[i] = x.T
    v_t_ref[...] = v.T


def _samp_wkv_kernel(s_ref, vt_ref, v_ref, so_ref, y_ref):
    w, kk, b, k2, r = [vt_ref[i] for i in range(5)]
    for j in range(s_ref.shape[0]):
        s = s_ref[j]
        sa = jnp.sum(s * (-kk), axis=0, keepdims=True)
        s = s * w + sa * b + v_ref[j:j + 1, :] * k2
        so_ref[j] = s
        y_ref[j:j + 1, :] = jnp.sum(s * r, axis=0, keepdims=True)


def _samp_state_kernel(sret_ref, rows_ref, vr_ref, sdec_ref, sret_o_ref, cross_ref):
    q, k = [rows_ref[:, :, i:i + 1, :] for i in range(2)]
    eye = (lax.broadcasted_iota(jnp.int32, (HEAD_DIM, HEAD_DIM), 0)
           == lax.broadcasted_iota(jnp.int32, (HEAD_DIM, HEAD_DIM), 1))

    def to_col(row):
        return jnp.sum(jnp.where(eye, row, 0.0), axis=3, keepdims=True)

    s = sret_ref[...]
    cross_ref[...] = jnp.sum(s * to_col(q), axis=2, keepdims=True)
    sret_o_ref[...] = s * sdec_ref[...][None] + to_col(k) * vr_ref[...]


def _samp_post_kernel(vec_ref, qk_ref, cross_ref, y_t_ref, zv_ref, zg_ref, qdec_ref, gng_ref, rk_ref,
                      lng_ref, lnb_ref, yab_ref, yc_ref):
    yab_ref[:, 0:D_A] = vec_ref[0].astype(BF16)
    ob = qk_ref[...] * zv_ref[...] + cross_ref[...] * qdec_ref[...]
    for h in range(N_HEADS):
        cols = slice(h * DV_B, (h + 1) * DV_B)
        yb = _rms(ob[:, cols]) * gng_ref[:, cols] * _silu(zg_ref[:, cols])
        yab_ref[:, D_A + h * DV_B:D_A + (h + 1) * DV_B] = yb.astype(BF16)
    out = _rwkv_output(y_t_ref[...].T, vec_ref[4], vec_ref[5], vec_ref[6], vec_ref[7], rk_ref[...],
                       lng_ref[...], lnb_ref[...])
    yc_ref[...] = out.astype(BF16)


def _mix_sample(z, s_ret_all, s_wkv_t_all, shift, l, p):
    bs = z.shape[0]
    bb = math.gcd(bs, SAMPLE_RET_SEQS)
    vb = SAMPLE_WKV_VBLOCK
    full = lambda a: pl.BlockSpec(a.shape, lambda *_: (0,) * a.ndim)
    cos_t, sin_t = _rope_tables(jnp.full((1,), PAST_LEN, F32))
    lg = _retention_log_decay()
    ws0 = jnp.repeat(p['sg_w_s'][l][:, 0, 0], HEAD_DIM).reshape(1, D_A)
    bs0 = jnp.repeat(p['sg_b_s'][l][:, 0], HEAD_DIM).reshape(1, D_A)
    gain = p['sg_v_gain'][l].reshape(1, D_A)
    prep_args = [z, shift, cos_t, sin_t, ws0, bs0, gain] + _rwkv_params(l, p)
    prep_shapes = [(SAMPLE_SLOTS, bs, D_A), (bs, D_BV), (5, D_C, bs), (D_C, bs)]
    vec, qk, vt, v_t = pl.pallas_call(
        _samp_prep_kernel,
        grid=(1,),
        in_specs=[full(a) for a in prep_args],
        out_specs=[pl.BlockSpec(s, lambda i, n=len(s): (0,) * n) for s in prep_shapes],
        out_shape=[jax.ShapeDtypeStruct(s, F32) for s in prep_shapes],
        compiler_params=_cparams(("arbitrary",)),
        name="sample_prep",
    )(*prep_args)

    s_wkv_new_t, y_t = pl.pallas_call(
        _samp_wkv_kernel,
        grid=(N_HEADS, HEAD_DIM // vb),
        in_specs=[pl.BlockSpec((None, None, vb, HEAD_DIM, bs), lambda h, i: (l, h, i, 0, 0)),
                  pl.BlockSpec((5, None, HEAD_DIM, bs), lambda h, i: (0, h, 0, 0)),
                  pl.BlockSpec((None, vb, bs), lambda h, i: (h, i, 0))],
        out_specs=[pl.BlockSpec((None, vb, HEAD_DIM, bs), lambda h, i: (h, i, 0, 0)),
                   pl.BlockSpec((None, vb, bs), lambda h, i: (h, i, 0))],
        out_shape=[jax.ShapeDtypeStruct((N_HEADS, HEAD_DIM, HEAD_DIM, bs), F32),
                   jax.ShapeDtypeStruct((N_HEADS, HEAD_DIM, bs), F32)],
        compiler_params=_cparams(("parallel", "parallel")),
        name="sample_wkv",
    )(s_wkv_t_all, vt.reshape(5, N_HEADS, HEAD_DIM, bs), v_t.reshape(N_HEADS, HEAD_DIM, bs))

    zv = z[:, 4 * D_A:4 * D_A + D_BV]
    zg = z[:, 4 * D_A + D_BV:Z_AB]
    sdec_t = jnp.broadcast_to(jnp.exp(lg)[:, None, None], (N_HEADS, HEAD_DIM, DV_B))
    rows = jnp.transpose(vec[2:4].reshape(2, bs, N_HEADS, HEAD_DIM), (1, 2, 0, 3))
    blk = lambda shape: pl.BlockSpec((bb,) + shape[1:], lambda i: (i, 0, 0, 0))
    out_shapes = [jax.ShapeDtypeStruct((bs, N_HEADS, HEAD_DIM, DV_B), F32),
                  jax.ShapeDtypeStruct((bs, N_HEADS, 1, DV_B), F32)]
    s_ret_new, cross = pl.pallas_call(
        _samp_state_kernel,
        grid=(bs // bb,),
        in_specs=[pl.BlockSpec((None, bb) + s_ret_all.shape[2:], lambda i: (l, i, 0, 0, 0)),
                  blk(rows.shape), blk((bs, N_HEADS, 1, DV_B)),
                  pl.BlockSpec(sdec_t.shape, lambda i: (0, 0, 0))],
        out_specs=[blk(a.shape) for a in out_shapes],
        out_shape=out_shapes,
        compiler_params=_cparams(("parallel",)),
        name="sample_state",
    )(s_ret_all, rows, zv.reshape(bs, N_HEADS, 1, DV_B), sdec_t)

    qdec_t = jnp.repeat(jnp.exp(lg), DV_B).reshape(1, D_BV)
    post_args = [vec, qk, cross.reshape(bs, D_BV), y_t.reshape(D_C, bs), zv, zg, qdec_t,
                 p['ret_gn_g'][l].reshape(1, D_BV), p['rw_r_k'][l].reshape(1, D_C),
                 p['rw_ln_g'][l].reshape(1, D_C), p['rw_ln_b'][l].reshape(1, D_C)]
    yab, yc = pl.pallas_call(
        _samp_post_kernel,
        grid=(1,),
        in_specs=[full(a) for a in post_args],
        out_specs=[pl.BlockSpec((bs, D_AB), lambda i: (0, 0)), pl.BlockSpec((bs, D_C), lambda i: (0, 0))],
        out_shape=[jax.ShapeDtypeStruct((bs, D_AB), BF16), jax.ShapeDtypeStruct((bs, D_C), BF16)],
        compiler_params=_cparams(("arbitrary",)),
        name="sample_post",
    )(*post_args)
    return yab, yc, s_ret_new, s_wkv_new_t, vec[1]


def _row_tile(n_rows_per_seq):
    for t in (512, 256, 128):
        if n_rows_per_seq % t == 0:
            return t
    raise ValueError("sequence length must be a multiple of 128")


def kernel(x_prompt, x_sample, state_ret, state_wkv, state_shift, c_prompt, c_sample,
           w_ada, b_ada, w_ffn1_in, w_ffn1_out, w_in, w_out, w_ffn2_in, w_ffn2_out,
           sg_v_gain, sg_w_s, sg_b_s, ret_gn_g,
           rw_mu, rw_w0, rw_w2, rw_a0, rw_a2, rw_g2, rw_k_k, rw_k_a, rw_r_k, rw_ln_g, rw_ln_b,
           final_g):
    p = dict(sg_v_gain=sg_v_gain, sg_w_s=sg_w_s, sg_b_s=sg_b_s, ret_gn_g=ret_gn_g, rw_mu=rw_mu,
             rw_w0=rw_w0, rw_w2=rw_w2, rw_a0=rw_a0, rw_a2=rw_a2, rw_g2=rw_g2, rw_k_k=rw_k_k,
             rw_k_a=rw_k_a, rw_r_k=rw_r_k, rw_ln_g=rw_ln_g, rw_ln_b=rw_ln_b, final_g=final_g)
    wb = dict(ffn1_in=w_ffn1_in.astype(BF16), ffn1_out=w_ffn1_out.astype(BF16), w_in=w_in.astype(BF16),
              w_out=w_out.astype(BF16), ffn2_in=w_ffn2_in.astype(BF16), ffn2_out=w_ffn2_out.astype(BF16))
    bp, t, _ = x_prompt.shape
    bs = x_sample.shape[0]
    assert x_sample.shape[1] == 1 and t % CHUNK == 0

    bp_pad = -(-bp // SUBLANES) * SUBLANES
    assert bs % bp_pad == 0
    c_all = jnp.concatenate([c_sample, c_prompt, jnp.zeros((bp_pad - bp, D_MODEL), F32)], axis=0)
    mod = _ada(c_all, w_ada, b_ada)

    def prompt_mixer(z, l):
        yab, s_ret = _mix_ab_prompt(z, bp, t, l, p)
        yc, s_wkv = _rwkv_prompt(z, bp, t, l, p)
        last = z.reshape(bp, t, D_IN)[:, t - 1, Z_AB:]
        return yab, yc, (s_ret, s_wkv, last)

    tile_p = _row_tile(t)
    state_wkv_t = jnp.transpose(state_wkv, (0, 2, 3, 4, 1))

    def sample_mixer(z, l):
        yab, yc, s_ret, s_wkv_t, vn = _mix_sample(z, state_ret, state_wkv_t, state_shift[l], l, p)
        return yab, yc, (s_ret, s_wkv_t, z[:, Z_AB:], vn)

    n_tiles_p = bp * t // tile_p
    xs = [x_prompt.reshape(bp * t, D_MODEL), x_sample.reshape(bs, D_MODEL)]
    ex_p, ex_s = [], []
    for l in range(DEPTH):
        rows = [_Rows(mod, l, bp * t, tile_p, tiles_per_seq=t // tile_p, n_seq=bp_pad, seq_block=bs // bp_pad),
                _Rows(mod, l, bs, bs, start=n_tiles_p)]
        xs = _ffn(xs, rows, l, 0, wb['ffn1_in'], wb['ffn1_out'])
        z_p, z_s = _proj_in(xs, rows, l, wb['w_in'])
        yab_p, yc_p, extra_p = prompt_mixer(z_p, l)
        yab_s, yc_s, extra_s = sample_mixer(z_s, l)
        xs = _ffn(xs, rows, l, 6, wb['ffn2_in'], wb['ffn2_out'],
                  final_g=final_g if l == DEPTH - 1 else None,
                  mixer=([yab_p, yab_s], [yc_p, yc_s], wb['w_out']))
        ex_p.append(extra_p)
        ex_s.append(extra_s)
    y_p, y_s = xs

    stack = lambda ex, i: jnp.stack([e[i] for e in ex])
    return (y_p.reshape(bp, t, D_MODEL), y_s.reshape(bs, 1, D_MODEL),
            stack(ex_p, 0), stack(ex_p, 1), stack(ex_p, 2),
            stack(ex_s, 0), jnp.transpose(stack(ex_s, 1), (0, 4, 1, 2, 3)), stack(ex_s, 2),
            stack(ex_s, 3).reshape(DEPTH, bs, 1, D_A))
```

```python
import functools
import math

import jax
import jax.numpy as jnp
from jax import lax
from jax.experimental import pallas as pl
from jax.experimental.pallas import tpu as pltpu

F32 = jnp.float32
BF16 = jnp.bfloat16

D_MODEL = 1024
DEPTH = 4
N_MOD = 9
D_FF = 2816
N_HEADS = 4
HEAD_DIM = 64
DV_B = 128
CHUNK = 128
RW_CHUNK = 64
PROJ_IN_TILE = 1024
AB_SEQS_PER_STEP = 8
RW_SEQS_PER_STEP = 8
D_A = N_HEADS * HEAD_DIM
D_BV = N_HEADS * DV_B
D_C = N_HEADS * HEAD_DIM
C_IN = 1024
D_IN = 3072
Z_AB = 2048
D_AB = D_A + D_BV
ROPE_BASE = 10000.0
NORM_EPS = 1e-6
GN_EPS = 64e-5
PAST_LEN = 16384

VMEM_LIMIT_BYTES = 56 * 1024 * 1024
SUBLANES = 8
MXU_WIDTH = 256
FFN_CHUNKS = ((0, 6 * MXU_WIDTH), (6 * MXU_WIDTH, 5 * MXU_WIDTH))


def _cparams(sem):
    return pltpu.CompilerParams(dimension_semantics=sem, vmem_limit_bytes=VMEM_LIMIT_BYTES)


def _dot(a, b):
    return jnp.dot(a.astype(BF16), b.astype(BF16), preferred_element_type=F32)


def _dot_t(a, b):
    return lax.dot_general(a.astype(BF16), b.astype(BF16), (((1,), (1,)), ((), ())),
                           preferred_element_type=F32)


def _split2(x):
    hi = x.astype(BF16)
    lo = (x - hi.astype(F32)).astype(BF16)
    return hi, lo


def _split3(x):
    hi = x.astype(BF16)
    r1 = x - hi.astype(F32)
    mid = r1.astype(BF16)
    lo = (r1 - mid.astype(F32)).astype(BF16)
    return hi, mid, lo


def _dot_sel(x, sel):
    m = x.shape[0]
    out = jnp.dot(jnp.concatenate(_split2(x), axis=0), sel, preferred_element_type=F32)
    return out[0:m] + out[m:2 * m]


def _dot3_general(a, b, dn):
    ah, al = a if isinstance(a, tuple) else _split2(a)
    bh, bl = b if isinstance(b, tuple) else _split2(b)
    m = ah.shape[0]
    top = lax.dot_general(jnp.concatenate([ah, al], axis=0), bh, dn, preferred_element_type=F32)
    return top[0:m] + top[m:2 * m] + lax.dot_general(ah, bl, dn, preferred_element_type=F32)


_DN_AB = (((1,), (0,)), ((), ()))
_DN_ABT = (((1,), (1,)), ((), ()))


def _dot3_with_rider(a, rider, b, dn):
    ah, al = a if isinstance(a, tuple) else _split2(a)
    bh, bl = b if isinstance(b, tuple) else _split2(b)
    m, n = ah.shape[0], rider.shape[0]
    top = lax.dot_general(jnp.concatenate([ah, al, rider.astype(BF16)], axis=0), bh, dn,
                          preferred_element_type=F32)
    main = top[0:m] + top[m:2 * m] + lax.dot_general(ah, bl, dn, preferred_element_type=F32)
    return main, top[2 * m:2 * m + n]


def _dot3(a, b):
    return _dot3_general(a, b, _DN_AB)


def _rms(x, eps=NORM_EPS):
    return x * lax.rsqrt(jnp.mean(x * x, axis=-1, keepdims=True) + eps)


def _silu(x):
    return x * jax.nn.sigmoid(x)


def _gelu(x):
    c = math.sqrt(2.0 / math.pi)
    return (0.5 * x) * (1.0 + jnp.tanh(x * (c + (c * 0.044715) * (x * x))))


def _head_of(shape, dim, width):
    return lax.broadcasted_iota(jnp.int32, shape, dim) // width


def _group_sel(n_in, n_out, w_in, w_out, scale=1.0):
    gi = _head_of((n_in, n_out), 0, w_in)
    go = _head_of((n_in, n_out), 1, w_out)
    return jnp.where(gi == go, scale, 0.0).astype(BF16)


def _stack_heads(x, width):
    lane_head = _head_of(x.shape, 1, width)
    return jnp.concatenate([jnp.where(lane_head == h, x, 0.0) for h in range(N_HEADS)], axis=0)


def _rotate_half(x, cos_t, sin_t):
    n = x.shape[1]
    half = HEAD_DIM // 2
    lane = lax.broadcasted_iota(jnp.int32, x.shape, 1)
    fwd = pltpu.roll(x, n - half, axis=1)
    bwd = pltpu.roll(x, half, axis=1)
    partner = jnp.where((lane % HEAD_DIM) < half, fwd, bwd)
    return x * cos_t + partner * sin_t


ADA_VECS_PER_STEP = 3


def _ada_kernel(c_ref, w_ref, b_ref, o_ref):
    cs = _silu(c_ref[...]).astype(BF16)
    for j in range(ADA_VECS_PER_STEP):
        cols = slice(j * D_MODEL, (j + 1) * D_MODEL)
        o_ref[j] = _dot(cs, w_ref[:, cols]) + b_ref[:, cols]


def _ada(c_all, w_ada, b_ada):
    bt = c_all.shape[0]
    nv = ADA_VECS_PER_STEP
    return pl.pallas_call(
        _ada_kernel,
        grid=(DEPTH, N_MOD // nv),
        in_specs=[pl.BlockSpec((bt, D_MODEL), lambda l, j: (0, 0)),
                  pl.BlockSpec((None, D_MODEL, nv * D_MODEL), lambda l, j: (l, 0, j)),
                  pl.BlockSpec((None, 1, nv * D_MODEL), lambda l, j: (l, 0, j))],
        out_specs=pl.BlockSpec((None, nv, bt, D_MODEL), lambda l, j: (l, j, 0, 0)),
        out_shape=jax.ShapeDtypeStruct((DEPTH, N_MOD, bt, D_MODEL), F32),
        compiler_params=_cparams(("parallel", "parallel")),
        name="ada_mod",
    )(c_all, w_ada, b_ada.reshape(DEPTH, 1, N_MOD * D_MODEL))


class _Rows:
    def __init__(self, mod, layer, n_rows, tile, tiles_per_seq=None, n_seq=None, seq_block=None, start=0):
        self.mod, self.layer, self.n_rows, self.tile = mod, layer, n_rows, tile
        self.n_tiles = n_rows // tile
        self.tiles_per_seq, self.n_seq, self.seq_block = tiles_per_seq, n_seq, seq_block
        self.start = start
        assert tiles_per_seq is None or start == 0

    def _tile_index(self, i):
        return jnp.clip(i - self.start, 0, self.n_tiles - 1)

    def mod_spec(self, j):
        l = self.layer
        if self.tiles_per_seq is None:
            return pl.BlockSpec((None, None, self.tile, D_MODEL), lambda i: (l, j, self._tile_index(i), 0))
        blk = self.seq_block
        return pl.BlockSpec((None, None, self.n_seq, D_MODEL), lambda i: (l, j, blk, 0))

    def row_spec(self, width):
        return pl.BlockSpec((self.tile, width), lambda i: (self._tile_index(i), 0))


def _mod_row(ref, tiles_per_seq):
    if tiles_per_seq is None:
        return ref[...]
    return ref[pl.ds(pl.program_id(0) // tiles_per_seq, 1), :]


def _in_group(rows_groups, gi):
    if len(rows_groups) == 1:
        return None
    i = pl.program_id(0)
    start, n_tiles = rows_groups[gi][1], rows_groups[gi][2]
    return jnp.logical_and(i >= start, i < start + n_tiles)


def _ffn_kernel(*refs, final, mixer_out, groups):
    refs = list(refs)
    o_refs = [refs.pop() for _ in groups][::-1]
    wi_ref, wo_ref = refs[:2]
    refs = refs[2:]
    mix_w = None
    if mixer_out:
        mix_w, refs = refs[:2], refs[2:]
    fg_ref = None
    if final:
        fg_ref, refs = refs[0], refs[1:]
    per_group = len(refs) // len(groups)
    for gi, group in enumerate(groups):
        g_refs = refs[gi * per_group:(gi + 1) * per_group]
        body = functools.partial(_ffn_rows, g_refs, wi_ref, wo_ref, mix_w, fg_ref, o_refs[gi], group[0])
        pred = _in_group(groups, gi)
        if pred is None:
            body()
        else:
            pl.when(pred)(body)


def _ffn_rows(g_refs, wi_ref, wo_ref, mix_w, fg_ref, o_ref, tiles_per_seq):
    x_ref, sh_ref, sc_ref, g_ref = g_refs[:4]
    mod = functools.partial(_mod_row, tiles_per_seq=tiles_per_seq)
    x = x_ref[...]
    if mix_w is not None:
        gm_ref, yab_ref, yc_ref = g_refs[4:7]
        wab_ref, wc_ref = mix_w
        x = x + mod(gm_ref) * (jnp.dot(yab_ref[...], wab_ref[...], preferred_element_type=F32)
                               + jnp.dot(yc_ref[...], wc_ref[...], preferred_element_type=F32))
    final = fg_ref is not None
    hb = (_rms(x) * (1.0 + mod(sc_ref)) + mod(sh_ref)).astype(BF16)
    y = None
    for c0, cw in FFN_CHUNKS:
        gate = jnp.dot(hb, wi_ref[:, c0:c0 + cw], preferred_element_type=F32)
        up = jnp.dot(hb, wi_ref[:, D_FF + c0:D_FF + c0 + cw], preferred_element_type=F32)
        part = jnp.dot((_silu(gate) * up).astype(BF16), wo_ref[c0:c0 + cw, :], preferred_element_type=F32)
        y = part if y is None else y + part
    out = x + 0.5 * mod(g_ref) * y
    if final:
        out = _rms(out) * fg_ref[...]
    o_ref[...] = out


def _resident(shape, index_map):
    return pl.BlockSpec(shape, index_map, pipeline_mode=pl.Buffered(1))


def _group_meta(rows_list):
    return tuple((r.tiles_per_seq, r.start, r.n_tiles) for r in rows_list)


def _ffn(xs, rows_list, l, jm, w_in_b, w_out_b, final_g=None, mixer=None):
    final = final_g is not None
    in_specs = [_resident((None, D_MODEL, 2 * D_FF), lambda i: (l, 0, 0)),
                _resident((None, D_FF, D_MODEL), lambda i: (l, 0, 0))]
    args = [w_in_b, w_out_b]
    if mixer is not None:
        in_specs += [_resident((None, D_AB, D_MODEL), lambda i: (l, 0, 0)),
                     _resident((None, D_C, D_MODEL), lambda i: (l, D_AB // D_C, 0))]
        args += [mixer[2], mixer[2]]
    if final:
        in_specs.append(pl.BlockSpec((1, D_MODEL), lambda i: (0, 0)))
        args.append(final_g.reshape(1, D_MODEL))
    for gi, (x, rows) in enumerate(zip(xs, rows_list)):
        in_specs += [rows.row_spec(D_MODEL), rows.mod_spec(jm), rows.mod_spec(jm + 1), rows.mod_spec(jm + 2)]
        args += [x, rows.mod, rows.mod, rows.mod]
        if mixer is not None:
            in_specs += [rows.mod_spec(5), rows.row_spec(D_AB), rows.row_spec(D_C)]
            args += [rows.mod, mixer[0][gi], mixer[1][gi]]
    return pl.pallas_call(
        functools.partial(_ffn_kernel, final=final, mixer_out=mixer is not None,
                          groups=_group_meta(rows_list)),
        grid=(sum(r.n_tiles for r in rows_list),),
        in_specs=in_specs,
        out_specs=[r.row_spec(D_MODEL) for r in rows_list],
        out_shape=[jax.ShapeDtypeStruct((r.n_rows, D_MODEL), F32) for r in rows_list],
        compiler_params=_cparams(("arbitrary",)),
        name="ffn_final" if final else "ffn",
    )(*args)


def _proj_in_kernel(*refs, groups):
    refs = list(refs)
    z_refs = [refs.pop() for _ in groups][::-1]
    w_ref, refs = refs[0], refs[1:]
    for gi, group in enumerate(groups):
        body = functools.partial(_proj_in_rows, *refs[3 * gi:3 * gi + 3], w_ref, z_refs[gi], group[0])
        pred = _in_group(groups, gi)
        if pred is None:
            body()
        else:
            pl.when(pred)(body)


def _proj_in_rows(x_ref, sh_ref, sc_ref, w_ref, z_ref, tiles_per_seq):
    sh, sc = _mod_row(sh_ref, tiles_per_seq), _mod_row(sc_ref, tiles_per_seq)
    h = (_rms(x_ref[...]) * (1.0 + sc) + sh).astype(BF16)
    z_ref[...] = jnp.dot(h, w_ref[...], preferred_element_type=F32)


def _proj_in(xs, rows_list, l, w_in_b):
    in_specs = [_resident((None, D_MODEL, D_IN), lambda i: (l, 0, 0))]
    args = [w_in_b]
    for x, rows in zip(xs, rows_list):
        in_specs += [rows.row_spec(D_MODEL), rows.mod_spec(3), rows.mod_spec(4)]
        args += [x, rows.mod, rows.mod]
    return pl.pallas_call(
        functools.partial(_proj_in_kernel, groups=_group_meta(rows_list)),
        grid=(sum(r.n_tiles for r in rows_list),),
        in_specs=in_specs,
        out_specs=[r.row_spec(D_IN) for r in rows_list],
        out_shape=[jax.ShapeDtypeStruct((r.n_rows, D_IN), F32) for r in rows_list],
        compiler_params=_cparams(("arbitrary",)),
        name="proj_in",
    )(*args)


def _retention_log_decay():
    return jnp.log1p(-jnp.exp(jnp.linspace(math.log(1.0 / 32.0), math.log(1.0 / 512.0), N_HEADS, dtype=F32)))


def _rope_tables(pos):
    half = HEAD_DIM // 2
    inv = ROPE_BASE ** (-jnp.arange(half, dtype=F32) / half)
    ang = pos[:, None] * inv[None, :]
    cos, sin = jnp.cos(ang), jnp.sin(ang)
    cos_t = jnp.tile(jnp.concatenate([cos, cos], axis=-1), (1, N_HEADS))
    sin_t = jnp.tile(jnp.concatenate([-sin, sin], axis=-1), (1, N_HEADS))
    return cos_t, sin_t


def _retention_chunk_tables(c):
    lg = _retention_log_decay()
    idx = jnp.arange(c, dtype=F32)
    diff = idx[:, None] - idx[None, :]
    dmask = jnp.where(diff[None] >= 0, jnp.exp(jnp.maximum(diff, 0.0)[None] * lg[:, None, None]), 0.0)
    q_dec = jnp.exp((idx[:, None] + 1.0) * lg[None, :])
    k_dec = jnp.exp((c - 1.0 - idx)[:, None] * lg[None, :])
    s_dec = jnp.exp(c * lg)
    qdec_t = jnp.repeat(q_dec, DV_B, axis=1)
    kdec_t = jnp.repeat(k_dec, HEAD_DIM, axis=1)
    sdec_t = jnp.broadcast_to(jnp.repeat(s_dec, HEAD_DIM)[:, None], (N_HEADS * HEAD_DIM, DV_B))
    return dmask, qdec_t, kdec_t, sdec_t


def _mix_ab_kernel(z_ref, cos_ref, sin_ref, ws_ref, bs_ref, gain_ref, dmask_ref, qdec_ref, kdec_ref,
                   sdec_ref, gng_ref, yab_ref, s_ref):
    @pl.when(pl.program_id(1) == 0)
    def _():
        s_ref[...] = jnp.zeros_like(s_ref)

    seqs = range(z_ref.shape[0])
    mean_sel = _group_sel(D_A, D_A, HEAD_DIM, HEAD_DIM, 1.0 / HEAD_DIM)
    gv = [_gelu(z_ref[i, :, D_A:2 * D_A]) for i in seqs]
    vn = [g * lax.rsqrt(_dot_sel(g * g, mean_sel) + NORM_EPS) * gain_ref[...] for g in gv]
    lane_head = _head_of((CHUNK, D_A), 1, HEAD_DIM)
    row = lax.broadcasted_iota(jnp.int32, (CHUNK, CHUNK), 0)
    col = lax.broadcasted_iota(jnp.int32, (CHUNK, CHUNK), 1)
    mixed = [bs_ref[...] for _ in seqs]
    vn_b = [v.astype(BF16) for v in vn]
    for h in range(N_HEADS):
        w_m = jnp.where(row >= col, ws_ref[h], 0.0).astype(BF16)
        mixed = [m + jnp.dot(w_m, jnp.where(lane_head == h, v, jnp.zeros_like(v)), preferred_element_type=F32)
                 for m, v in zip(mixed, vn_b)]
    for i in seqs:
        yab_ref[i, :, 0:D_A] = (_gelu(z_ref[i, :, 0:D_A]) * mixed[i]).astype(BF16)

    q = [_rotate_half(z_ref[i, :, 2 * D_A:3 * D_A], cos_ref[...], sin_ref[...]).astype(BF16) for i in seqs]
    k = [_rotate_half(z_ref[i, :, 3 * D_A:4 * D_A], cos_ref[...], sin_ref[...]) * (HEAD_DIM ** -0.5)
         for i in seqs]
    s_all = [s_ref[i] for i in seqs]
    kd_t = [(k_i * kdec_ref[...]).T for k_i in k]
    kb = [k_i.astype(BF16) for k_i in k]
    sb = [s.astype(BF16) for s in s_all]
    for h in range(N_HEADS):
        cols_v = slice(4 * D_A + h * DV_B, 4 * D_A + (h + 1) * DV_B)
        cols_g = slice(4 * D_A + D_BV + h * DV_B, 4 * D_A + D_BV + (h + 1) * DV_B)
        cols_h = slice(h * DV_B, (h + 1) * DV_B)
        rows_h = slice(h * HEAD_DIM, (h + 1) * HEAD_DIM)
        v_h = [z_ref[i, :, cols_v] for i in seqs]
        q_h = [jnp.where(lane_head == h, q_i, jnp.zeros_like(q_i)) for q_i in q]
        scores = [_dot_t(q_i, k_i) * dmask_ref[h] for q_i, k_i in zip(q_h, kb)]
        inner = [_dot(sc, v_i) for sc, v_i in zip(scores, v_h)]
        cross = [jnp.dot(q_i, s_i, preferred_element_type=F32) * qdec_ref[:, cols_h]
                 for q_i, s_i in zip(q_h, sb)]
        upd = [_dot(kd_i[rows_h, :], v_i) for kd_i, v_i in zip(kd_t, v_h)]
        for i in seqs:
            yb = _rms(inner[i] + cross[i]) * gng_ref[:, cols_h] * _silu(z_ref[i, :, cols_g])
            yab_ref[i, :, D_A + h * DV_B:D_A + (h + 1) * DV_B] = yb.astype(BF16)
            s_ref[i, rows_h, :] = s_all[i][rows_h, :] * sdec_ref[rows_h, :] + upd[i]


def _mix_ab_prompt(z, bsz, t, l, p):
    n_c = t // CHUNK
    pos = jnp.arange(t, dtype=F32)
    cos_t, sin_t = _rope_tables(pos)
    dmask, qdec_t, kdec_t, sdec_t = _retention_chunk_tables(CHUNK)
    bs_t = jnp.repeat(p['sg_b_s'][l].T, HEAD_DIM, axis=1)
    gain = p['sg_v_gain'][l].reshape(1, D_A)
    gng = p['ret_gn_g'][l].reshape(1, D_BV)
    const = lambda shape: pl.BlockSpec(shape, lambda b, c: (0,) * len(shape))
    nb = math.gcd(bsz, AB_SEQS_PER_STEP)
    yab, s_ret = pl.pallas_call(
        _mix_ab_kernel,
        grid=(bsz // nb, n_c),
        in_specs=[pl.BlockSpec((nb, CHUNK, Z_AB), lambda b, c: (b, c, 0)),
                  pl.BlockSpec((CHUNK, D_A), lambda b, c: (c, 0)),
                  pl.BlockSpec((CHUNK, D_A), lambda b, c: (c, 0)),
                  pl.BlockSpec((None, N_HEADS, CHUNK, CHUNK), lambda b, c: (l, 0, 0, 0)),
                  const((CHUNK, D_A)), const((1, D_A)), const((N_HEADS, CHUNK, CHUNK)),
                  const((CHUNK, D_BV)), const((CHUNK, D_A)), const((D_A, DV_B)), const((1, D_BV))],
        out_specs=[pl.BlockSpec((nb, CHUNK, D_AB), lambda b, c: (b, c, 0)),
                   pl.BlockSpec((nb, D_A, DV_B), lambda b, c: (b, 0, 0))],
        out_shape=[jax.ShapeDtypeStruct((bsz, t, D_AB), BF16),
                   jax.ShapeDtypeStruct((bsz, D_A, DV_B), F32)],
        compiler_params=_cparams(("parallel", "arbitrary")),
        name="mix_ab",
    )(z.reshape(bsz, t, D_IN), cos_t, sin_t, p['sg_w_s'], bs_t, gain, dmask, qdec_t, kdec_t, sdec_t, gng)
    return yab.reshape(bsz * t, D_AB), s_ret.reshape(bsz, N_HEADS, HEAD_DIM, DV_B)


def _softplus(x):
    return jnp.maximum(x, 0.0) + jnp.log(1.0 + jnp.exp(-jnp.abs(x)))


def _rwkv_vectors(zc, shifted, mu, w0, w2p, a0, a2p, g2, k_k, k_a):
    zs = zc + (shifted - zc) * mu
    r = zs[:, 0:D_C]
    k = zs[:, D_C:2 * D_C]
    v = zs[:, 2 * D_C:3 * D_C]
    xwa = zs[:, 3 * D_C:3 * D_C + 128]
    xg = zs[:, 3 * D_C + 128:C_IN]
    w_log = -_softplus(-(w0 + _dot(jnp.tanh(xwa), w2p))) - 0.5
    logw = -jnp.exp(w_log)
    a = jax.nn.sigmoid(a0 + _dot(xwa, a2p))
    g = _dot(jax.nn.sigmoid(xg), g2)
    kk = k * k_k
    sum_sel = _group_sel(D_C, D_C, HEAD_DIM, HEAD_DIM)
    kk = kk * lax.rsqrt(jnp.maximum(_dot_sel(kk * kk, sum_sel), 1e-24))
    k2 = k * (1.0 + (a - 1.0) * k_a)
    return r, k2, v, logw, a, g, kk


def _rwkv_output(y, r, k2, v, g, r_k, ln_g, ln_b):
    mean_sel = _group_sel(D_C, D_C, HEAD_DIM, HEAD_DIM, 1.0 / HEAD_DIM)
    sum_sel = _group_sel(D_C, D_C, HEAD_DIM, HEAD_DIM)
    yc = y - _dot_sel(y, mean_sel)
    var = _dot_sel(yc * yc, mean_sel)
    yn = yc * lax.rsqrt(var + GN_EPS) * ln_g + ln_b
    bonus = _dot_sel(r * k2 * r_k, sum_sel) * v
    return (yn + bonus) * g


def _rwkv_kernel(zc_ref, mu_ref, w0_ref, w2p_ref, a0_ref, a2p_ref, g2_ref, kk_ref, ka_ref, rk_ref,
                 lng_ref, lnb_ref, yc_ref, s_ref, prev_ref):
    @pl.when(pl.program_id(1) == 0)
    def _():
        s_ref[...] = jnp.zeros_like(s_ref)
        prev_ref[...] = jnp.zeros_like(prev_ref)

    params = [ref[...] for ref in (mu_ref, w0_ref, w2p_ref, a0_ref, a2p_ref, g2_ref, kk_ref, ka_ref,
                                   rk_ref, lng_ref, lnb_ref)]
    n_seq = zc_ref.shape[0]
    outs, s_news, lasts = _rwkv_chunks([zc_ref[i] for i in range(n_seq)],
                                       [prev_ref[i] for i in range(n_seq)],
                                       [s_ref[i] for i in range(n_seq)], params)
    for i in range(n_seq):
        yc_ref[i] = outs[i]
        s_ref[i] = s_news[i]
        prev_ref[i] = lasts[i]


def _each(fn, *lists):
    return [fn(*args) for args in zip(*lists)]


def _rwkv_chunks(zcs, prevs, s_bds, params):
    mu, w0, w2p, a0, a2p, g2, k_k, k_a, r_k, ln_g, ln_b = params
    c_len = RW_CHUNK
    nw = N_HEADS * c_len
    row_id = lax.broadcasted_iota(jnp.int32, zcs[0].shape, 0)
    tri = jnp.where(lax.broadcasted_iota(jnp.int32, (c_len, c_len), 0)
                    >= lax.broadcasted_iota(jnp.int32, (c_len, c_len), 1), 1.0, 0.0).astype(BF16)
    t_id = lax.broadcasted_iota(jnp.int32, (c_len, nw), 0)
    s_id = lax.broadcasted_iota(jnp.int32, (c_len, nw), 1) % c_len
    strict = s_id < t_id
    incl = s_id <= t_id
    bd = _head_of((nw, nw), 0, c_len) == _head_of((nw, nw), 1, c_len)

    n_seq = len(zcs)
    per_seq = lambda x: [x[i * c_len:(i + 1) * c_len] for i in range(n_seq)]
    shifted = _each(lambda zc, prev: jnp.where(row_id == 0, prev, pltpu.roll(zc, 1, axis=0)), zcs, prevs)
    vec_all = _rwkv_vectors(jnp.concatenate(zcs, axis=0), jnp.concatenate(shifted, axis=0),
                            mu, w0, w2p, a0, a2p, g2, k_k, k_a)
    r_all, k2_all, v_all, _, _, g_all, _ = vec_all
    r, k2, v, logw, a, g, kk = [per_seq(x) for x in vec_all]
    b = _each(lambda kk_i, a_i: kk_i * a_i, kk, a)

    def cumsum(lw):
        hi, mid, lo = _split3(lw)
        return (jnp.dot(tri, hi, preferred_element_type=F32) + jnp.dot(tri, mid, preferred_element_type=F32)
                + jnp.dot(tri, lo, preferred_element_type=F32))
    cum = _each(cumsum, logw)
    c_last = [c[c_len - 1:c_len, :] for c in cum]
    a_hat = _each(lambda kk_i, c, lw: -kk_i * jnp.exp(c - lw), kk, cum, logw)
    r_hat = _each(lambda r_i, c: r_i * jnp.exp(c), r, cum)
    def stacked_parts(x):
        return tuple(_stack_heads(part, HEAD_DIM) for part in _split2(x))

    def bk_parts(b_i, k_i, c):
        e_inv = jnp.exp(-c)
        return tuple(jnp.concatenate([bp, kp], axis=0)
                     for bp, kp in zip(stacked_parts(b_i * e_inv), stacked_parts(k_i * e_inv)))
    bk_t = _each(bk_parts, b, k2, cum)
    a_parts = [_split2(ah) for ah in a_hat]
    gram = _each(lambda ap, rh, bk: _dot3_with_rider(ap, rh, bk, _DN_ABT), a_parts, r_hat, bk_t)
    x_w = [jnp.where(strict, ga[:, 0:nw], 0.0) for ga, _ in gram]
    l_ak = [jnp.where(strict, ga[:, nw:2 * nw], 0.0) for ga, _ in gram]
    m_rb = [jnp.where(incl, gr[:, 0:nw], 0.0) for _, gr in gram]
    m_rk = [jnp.where(incl, gr[:, nw:2 * nw], 0.0) for _, gr in gram]

    v_bd = [stacked_parts(v_i) for v_i in v]
    ar_s = _each(lambda ap, rh, s: _dot3_with_rider(ap, rh, s, _DN_ABT), a_parts, r_hat, s_bds)
    lm_v = _each(lambda l, m, vb: _dot3_with_rider(l, m, vb, _DN_AB), l_ak, m_rk, v_bd)
    w_rhs = _each(lambda s_part, v_part: s_part[0] + v_part[0], ar_s, lm_v)

    def expand(m_w):
        return jnp.where(bd, jnp.concatenate([m_w] * N_HEADS, axis=0), jnp.zeros((), m_w.dtype))
    power = [x.astype(BF16) for x in x_w]
    t_inv = [jnp.where(s_id == t_id, 1.0, x) for x in x_w]
    for _ in range(5):
        power = [jnp.dot(pw, expand(pw), preferred_element_type=F32).astype(BF16) for pw in power]
        t_inv = _each(lambda t, pw: t + jnp.dot(pw, expand(t.astype(BF16)), preferred_element_type=F32),
                      t_inv, power)
    t_inv = [t.astype(BF16) for t in t_inv]
    u = _each(lambda t, w: jnp.dot(t, _stack_heads(w.astype(BF16), HEAD_DIM), preferred_element_type=F32),
              t_inv, w_rhs)
    resid = _each(lambda w, u_i, x: w - u_i + _dot3(x, stacked_parts(u_i)), w_rhs, u, x_w)
    u = _each(lambda u_i, t, rs: u_i + jnp.dot(t, _stack_heads(rs.astype(BF16), HEAD_DIM),
                                               preferred_element_type=F32), u, t_inv, resid)
    u_bd = [stacked_parts(u_i) for u_i in u]

    y = _each(lambda s_part, v_part, mb, ub: s_part[1] + v_part[1] + jnp.dot(mb.astype(BF16), ub[0],
                                                                             preferred_element_type=F32),
              ar_s, lm_v, m_rb, u_bd)
    outs = per_seq(_rwkv_output(jnp.concatenate(y, axis=0), r_all, k2_all, v_all, g_all,
                                r_k, ln_g, ln_b).astype(BF16))

    bd_s = _head_of(s_bds[0].shape, 0, HEAD_DIM) == _head_of(s_bds[0].shape, 1, HEAD_DIM)

    def new_state(s, u_i, v_i, b_i, k_i, c, cl):
        e_tail = jnp.exp(cl - c)
        uv_t = jnp.concatenate([u_i, v_i], axis=0).T
        s_upd = _dot3(uv_t, jnp.concatenate([b_i * e_tail, k_i * e_tail], axis=0))
        return s * jnp.exp(cl) + jnp.where(bd_s, s_upd, 0.0)
    s_news = _each(new_state, s_bds, u, v, b, k2, cum, c_last)
    lasts = [zc[c_len - 1:c_len, :] for zc in zcs]
    return outs, s_news, lasts


def _rwkv_params(l, p):
    zeros = jnp.zeros((HEAD_DIM, D_C), F32)
    w2p = jnp.concatenate([p['rw_w2'][l], zeros], axis=0).astype(BF16)
    a2p = jnp.concatenate([zeros, p['rw_a2'][l]], axis=0).astype(BF16)
    row = lambda a: a.reshape(1, -1)
    return [row(p['rw_mu'][l]), row(p['rw_w0'][l]), w2p, row(p['rw_a0'][l]), a2p,
            p['rw_g2'][l].astype(BF16), row(p['rw_k_k'][l]), row(p['rw_k_a'][l])]


def _rwkv_prompt(z, bsz, t, l, p):
    n_c = t // RW_CHUNK
    params = _rwkv_params(l, p) + [p['rw_r_k'][l].reshape(1, D_C), p['rw_ln_g'][l].reshape(1, D_C),
                                   p['rw_ln_b'][l].reshape(1, D_C)]
    const = lambda a: pl.BlockSpec(a.shape, lambda b, c: (0,) * a.ndim)
    nb = math.gcd(bsz, RW_SEQS_PER_STEP)
    yc, s_bd = pl.pallas_call(
        _rwkv_kernel,
        grid=(bsz // nb, n_c),
        in_specs=[pl.BlockSpec((nb, RW_CHUNK, C_IN), lambda b, c: (b, c, Z_AB // C_IN))]
                 + [const(a) for a in params],
        out_specs=[pl.BlockSpec((nb, RW_CHUNK, D_C), lambda b, c: (b, c, 0)),
                   pl.BlockSpec((nb, D_C, D_C), lambda b, c: (b, 0, 0))],
        out_shape=[jax.ShapeDtypeStruct((bsz, t, D_C), BF16),
                   jax.ShapeDtypeStruct((bsz, D_C, D_C), F32)],
        scratch_shapes=[pltpu.VMEM((nb, 1, C_IN), F32)],
        compiler_params=_cparams(("parallel", "arbitrary")),
        name="rwkv_chunk",
    )(z.reshape(bsz, t, D_IN), *params)
    yc = yc.reshape(bsz * t, D_C)
    s_heads = jnp.stack([s_bd[:, h * HEAD_DIM:(h + 1) * HEAD_DIM, h * HEAD_DIM:(h + 1) * HEAD_DIM]
                         for h in range(N_HEADS)], axis=1)
    return yc, s_heads


SAMPLE_SLOTS = 8
SAMPLE_WKV_VBLOCK = 32
SAMPLE_RET_SEQS = 16


def _samp_prep_kernel(z_ref, prev_ref, cos_ref, sin_ref, ws0_ref, bs0_ref, gain_ref, mu_ref, w0_ref,
                      w2p_ref, a0_ref, a2p_ref, g2_ref, kk_ref, ka_ref, vec_ref, qk_ref, vt_ref, v_t_ref):
    gu = _gelu(z_ref[:, 0:D_A])
    gv = _gelu(z_ref[:, D_A:2 * D_A])
    mean_sel = _group_sel(D_A, D_A, HEAD_DIM, HEAD_DIM, 1.0 / HEAD_DIM)
    vn = gv * lax.rsqrt(_dot_sel(gv * gv, mean_sel) + NORM_EPS) * gain_ref[...]
    vec_ref[0] = gu * (ws0_ref[...] * vn + bs0_ref[...])
    vec_ref[1] = vn
    q = _rotate_half(z_ref[:, 2 * D_A:3 * D_A], cos_ref[...], sin_ref[...])
    k = _rotate_half(z_ref[:, 3 * D_A:4 * D_A], cos_ref[...], sin_ref[...]) * (HEAD_DIM ** -0.5)
    vec_ref[2] = q
    vec_ref[3] = k
    qk_ref[...] = _dot_sel(q * k, _group_sel(D_A, D_BV, HEAD_DIM, DV_B))
    r, k2, v, logw, a, g, kk = _rwkv_vectors(z_ref[:, Z_AB:D_IN], prev_ref[...], mu_ref[...], w0_ref[...],
                                             w2p_ref[...], a0_ref[...], a2p_ref[...], g2_ref[...],
                                             kk_ref[...], ka_ref[...])
    vec_ref[4] = r
    vec_ref[5] = k2
    vec_ref[6] = v
    vec_ref[7] = g
    for i, x in enumerate((jnp.exp(logw), kk, kk * a, k2, r)):
        vt_ref[i] = x.T
    v_t_ref[...] = v.T


def _samp_wkv_kernel(s_ref, vt_ref, v_ref, so_ref, y_ref):
    w, kk, b, k2, r = [vt_ref[i] for i in range(5)]
    for j in range(s_ref.shape[0]):
        s = s_ref[j]
        sa = jnp.sum(s * (-kk), axis=0, keepdims=True)
        s = s * w + sa * b + v_ref[j:j + 1, :] * k2
        so_ref[j] = s
        y_ref[j:j + 1, :] = jnp.sum(s * r, axis=0, keepdims=True)


def _samp_state_kernel(sret_ref, rows_ref, vr_ref, sdec_ref, sret_o_ref, cross_ref):
    q, k = [rows_ref[:, :, i:i + 1, :] for i in range(2)]
    eye = (lax.broadcasted_iota(jnp.int32, (HEAD_DIM, HEAD_DIM), 0)
           == lax.broadcasted_iota(jnp.int32, (HEAD_DIM, HEAD_DIM), 1))

    def to_col(row):
        return jnp.sum(jnp.where(eye, row, 0.0), axis=3, keepdims=True)

    s = sret_ref[...]
    cross_ref[...] = jnp.sum(s * to_col(q), axis=2, keepdims=True)
    sret_o_ref[...] = s * sdec_ref[...][None] + to_col(k) * vr_ref[...]


def _samp_post_kernel(vec_ref, qk_ref, cross_ref, y_t_ref, zv_ref, zg_ref, qdec_ref, gng_ref, rk_ref,
                      lng_ref, lnb_ref, yab_ref, yc_ref):
    yab_ref[:, 0:D_A] = vec_ref[0].astype(BF16)
    ob = qk_ref[...] * zv_ref[...] + cross_ref[...] * qdec_ref[...]
    for h in range(N_HEADS):
        cols = slice(h * DV_B, (h + 1) * DV_B)
        yb = _rms(ob[:, cols]) * gng_ref[:, cols] * _silu(zg_ref[:, cols])
        yab_ref[:, D_A + h * DV_B:D_A + (h + 1) * DV_B] = yb.astype(BF16)
    out = _rwkv_output(y_t_ref[...].T, vec_ref[4], vec_ref[5], vec_ref[6], vec_ref[7], rk_ref[...],
                       lng_ref[...], lnb_ref[...])
    yc_ref[...] = out.astype(BF16)


def _mix_sample(z, s_ret_all, s_wkv_t_all, shift, l, p):
    bs = z.shape[0]
    bb = math.gcd(bs, SAMPLE_RET_SEQS)
    vb = SAMPLE_WKV_VBLOCK
    full = lambda a: pl.BlockSpec(a.shape, lambda *_: (0,) * a.ndim)
    cos_t, sin_t = _rope_tables(jnp.full((1,), PAST_LEN, F32))
    lg = _retention_log_decay()
    ws0 = jnp.repeat(p['sg_w_s'][l][:, 0, 0], HEAD_DIM).reshape(1, D_A)
    bs0 = jnp.repeat(p['sg_b_s'][l][:, 0], HEAD_DIM).reshape(1, D_A)
    gain = p['sg_v_gain'][l].reshape(1, D_A)
    prep_args = [z, shift, cos_t, sin_t, ws0, bs0, gain] + _rwkv_params(l, p)
    prep_shapes = [(SAMPLE_SLOTS, bs, D_A), (bs, D_BV), (5, D_C, bs), (D_C, bs)]
    vec, qk, vt, v_t = pl.pallas_call(
        _samp_prep_kernel,
        grid=(1,),
        in_specs=[full(a) for a in prep_args],
        out_specs=[pl.BlockSpec(s, lambda i, n=len(s): (0,) * n) for s in prep_shapes],
        out_shape=[jax.ShapeDtypeStruct(s, F32) for s in prep_shapes],
        compiler_params=_cparams(("arbitrary",)),
        name="sample_prep",
    )(*prep_args)

    s_wkv_new_t, y_t = pl.pallas_call(
        _samp_wkv_kernel,
        grid=(N_HEADS, HEAD_DIM // vb),
        in_specs=[pl.BlockSpec((None, None, vb, HEAD_DIM, bs), lambda h, i: (l, h, i, 0, 0)),
                  pl.BlockSpec((5, None, HEAD_DIM, bs), lambda h, i: (0, h, 0, 0)),
                  pl.BlockSpec((None, vb, bs), lambda h, i: (h, i, 0))],
        out_specs=[pl.BlockSpec((None, vb, HEAD_DIM, bs), lambda h, i: (h, i, 0, 0)),
                   pl.BlockSpec((None, vb, bs), lambda h, i: (h, i, 0))],
        out_shape=[jax.ShapeDtypeStruct((N_HEADS, HEAD_DIM, HEAD_DIM, bs), F32),
                   jax.ShapeDtypeStruct((N_HEADS, HEAD_DIM, bs), F32)],
        compiler_params=_cparams(("parallel", "parallel")),
        name="sample_wkv",
    )(s_wkv_t_all, vt.reshape(5, N_HEADS, HEAD_DIM, bs), v_t.reshape(N_HEADS, HEAD_DIM, bs))

    zv = z[:, 4 * D_A:4 * D_A + D_BV]
    zg = z[:, 4 * D_A + D_BV:Z_AB]
    sdec_t = jnp.broadcast_to(jnp.exp(lg)[:, None, None], (N_HEADS, HEAD_DIM, DV_B))
    rows = jnp.transpose(vec[2:4].reshape(2, bs, N_HEADS, HEAD_DIM), (1, 2, 0, 3))
    blk = lambda shape: pl.BlockSpec((bb,) + shape[1:], lambda i: (i, 0, 0, 0))
    out_shapes = [jax.ShapeDtypeStruct((bs, N_HEADS, HEAD_DIM, DV_B), F32),
                  jax.ShapeDtypeStruct((bs, N_HEADS, 1, DV_B), F32)]
    s_ret_new, cross = pl.pallas_call(
        _samp_state_kernel,
        grid=(bs // bb,),
        in_specs=[pl.BlockSpec((None, bb) + s_ret_all.shape[2:], lambda i: (l, i, 0, 0, 0)),
                  blk(rows.shape), blk((bs, N_HEADS, 1, DV_B)),
                  pl.BlockSpec(sdec_t.shape, lambda i: (0, 0, 0))],
        out_specs=[blk(a.shape) for a in out_shapes],
        out_shape=out_shapes,
        compiler_params=_cparams(("parallel",)),
        name="sample_state",
    )(s_ret_all, rows, zv.reshape(bs, N_HEADS, 1, DV_B), sdec_t)

    qdec_t = jnp.repeat(jnp.exp(lg), DV_B).reshape(1, D_BV)
    post_args = [vec, qk, cross.reshape(bs, D_BV), y_t.reshape(D_C, bs), zv, zg, qdec_t,
                 p['ret_gn_g'][l].reshape(1, D_BV), p['rw_r_k'][l].reshape(1, D_C),
                 p['rw_ln_g'][l].reshape(1, D_C), p['rw_ln_b'][l].reshape(1, D_C)]
    yab, yc = pl.pallas_call(
        _samp_post_kernel,
        grid=(1,),
        in_specs=[full(a) for a in post_args],
        out_specs=[pl.BlockSpec((bs, D_AB), lambda i: (0, 0)), pl.BlockSpec((bs, D_C), lambda i: (0, 0))],
        out_shape=[jax.ShapeDtypeStruct((bs, D_AB), BF16), jax.ShapeDtypeStruct((bs, D_C), BF16)],
        compiler_params=_cparams(("arbitrary",)),
        name="sample_post",
    )(*post_args)
    return yab, yc, s_ret_new, s_wkv_new_t, vec[1]


def _row_tile(n_rows_per_seq):
    for t in (512, 256, 128):
        if n_rows_per_seq % t == 0:
            return t
    raise ValueError("sequence length must be a multiple of 128")


def kernel(x_prompt, x_sample, state_ret, state_wkv, state_shift, c_prompt, c_sample,
           w_ada, b_ada, w_ffn1_in, w_ffn1_out, w_in, w_out, w_ffn2_in, w_ffn2_out,
           sg_v_gain, sg_w_s, sg_b_s, ret_gn_g,
           rw_mu, rw_w0, rw_w2, rw_a0, rw_a2, rw_g2, rw_k_k, rw_k_a, rw_r_k, rw_ln_g, rw_ln_b,
           final_g):
    p = dict(sg_v_gain=sg_v_gain, sg_w_s=sg_w_s, sg_b_s=sg_b_s, ret_gn_g=ret_gn_g, rw_mu=rw_mu,
             rw_w0=rw_w0, rw_w2=rw_w2, rw_a0=rw_a0, rw_a2=rw_a2, rw_g2=rw_g2, rw_k_k=rw_k_k,
             rw_k_a=rw_k_a, rw_r_k=rw_r_k, rw_ln_g=rw_ln_g, rw_ln_b=rw_ln_b, final_g=final_g)
    wb = dict(ffn1_in=w_ffn1_in.astype(BF16), ffn1_out=w_ffn1_out.astype(BF16), w_in=w_in.astype(BF16),
              w_out=w_out.astype(BF16), ffn2_in=w_ffn2_in.astype(BF16), ffn2_out=w_ffn2_out.astype(BF16))
    bp, t, _ = x_prompt.shape
    bs = x_sample.shape[0]
    assert x_sample.shape[1] == 1 and t % CHUNK == 0

    bp_pad = -(-bp // SUBLANES) * SUBLANES
    assert bs % bp_pad == 0
    c_all = jnp.concatenate([c_sample, c_prompt, jnp.zeros((bp_pad - bp, D_MODEL), F32)], axis=0)
    mod = _ada(c_all, w_ada, b_ada)

    def prompt_mixer(z, l):
        yab, s_ret = _mix_ab_prompt(z, bp, t, l, p)
        yc, s_wkv = _rwkv_prompt(z, bp, t, l, p)
        last = z.reshape(bp, t, D_IN)[:, t - 1, Z_AB:]
        return yab, yc, (s_ret, s_wkv, last)

    tile_p = _row_tile(t)
    state_wkv_t = jnp.transpose(state_wkv, (0, 2, 3, 4, 1))

    def sample_mixer(z, l):
        yab, yc, s_ret, s_wkv_t, vn = _mix_sample(z, state_ret, state_wkv_t, state_shift[l], l, p)
        return yab, yc, (s_ret, s_wkv_t, z[:, Z_AB:], vn)

    n_tiles_p = bp * t // tile_p
    tile_pi = PROJ_IN_TILE if t % PROJ_IN_TILE == 0 else tile_p
    xs = [x_prompt.reshape(bp * t, D_MODEL), x_sample.reshape(bs, D_MODEL)]
    ex_p, ex_s = [], []
    for l in range(DEPTH):
        rows = [_Rows(mod, l, bp * t, tile_p, tiles_per_seq=t // tile_p, n_seq=bp_pad, seq_block=bs // bp_pad),
                _Rows(mod, l, bs, bs, start=n_tiles_p)]
        xs = _ffn(xs, rows, l, 0, wb['ffn1_in'], wb['ffn1_out'])
        rows_pi = [_Rows(mod, l, bp * t, tile_pi, tiles_per_seq=t // tile_pi, n_seq=bp_pad,
                         seq_block=bs // bp_pad),
                   _Rows(mod, l, bs, bs, start=bp * t // tile_pi)]
        z_p, z_s = _proj_in(xs, rows_pi, l, wb['w_in'])
        yab_p, yc_p, extra_p = prompt_mixer(z_p, l)
        yab_s, yc_s, extra_s = sample_mixer(z_s, l)
        xs = _ffn(xs, rows, l, 6, wb['ffn2_in'], wb['ffn2_out'],
                  final_g=final_g if l == DEPTH - 1 else None,
                  mixer=([yab_p, yab_s], [yc_p, yc_s], wb['w_out']))
        ex_p.append(extra_p)
        ex_s.append(extra_s)
    y_p, y_s = xs

    stack = lambda ex, i: jnp.stack([e[i] for e in ex])
    return (y_p.reshape(bp, t, D_MODEL), y_s.reshape(bs, 1, D_MODEL),
            stack(ex_p, 0), stack(ex_p, 1), stack(ex_p, 2),
            stack(ex_s, 0), jnp.transpose(stack(ex_s, 1), (0, 4, 1, 2, 3)), stack(ex_s, 2),
            stack(ex_s, 3).reshape(DEPTH, bs, 1, D_A))
```
